```python
import math
import jax, jax.numpy as jnp
from jax import lax
import numpy as np

D_MODEL = 1024
BATCH = 8
SEQ = 4096
DEPTH = 2

GRID_W = 64
CTX_LEN = 256
Q_BLOCK = 128
EPS = 1e-6
ROPE_THETA = 10000.0
DA_HEADS = 4
DA_QK_DIM = 64
DA_V_DIM = 2 * DA_QK_DIM
DA_WIDTH = DA_HEADS * DA_V_DIM
GQ_HEADS = 8
GQ_KV_HEADS = 2
GQ_GROUP = GQ_HEADS // GQ_KV_HEADS
GQ_DIM = 128
GQ_WIDTH = GQ_HEADS * GQ_DIM
HY_WIDTH = 512
HY_ORDER = 2
HY_SHORT = 3
HY_EMB = 33
HY_BANDS = (HY_EMB - 1) // 2
HY_FFN = 64
HY_TARGET = 1e-2
HY_FAST_PCT = 0.3
HY_SLOW_PCT = 1.5
HY_MAX_DECAY = math.log(HY_TARGET) / HY_FAST_PCT
HY_MIN_DECAY = math.log(HY_TARGET) / HY_SLOW_PCT
HY_U = (HY_ORDER + 1) * HY_WIDTH
D_MIX = DA_WIDTH + GQ_WIDTH + HY_WIDTH
COL_SIZES = (DA_HEADS * 2 * DA_QK_DIM, DA_HEADS * 2 * DA_QK_DIM, DA_WIDTH, DA_WIDTH,
             GQ_WIDTH, GQ_KV_HEADS * GQ_DIM, GQ_KV_HEADS * GQ_DIM, GQ_WIDTH,
             HY_U, HY_WIDTH)
D_IN = sum(COL_SIZES)

kernel_name = 'hymba_style_diffattn_gqa_hyena_dit'


def _rms(x, g):
    xf = x.astype(jnp.float32)
    y = xf * lax.rsqrt(jnp.mean(xf * xf, axis=-1, keepdims=True) + EPS)
    return (y * g.astype(jnp.float32)).astype(x.dtype)


def _axial_tables(L, head_dim):
    rows_n = L // GRID_W
    row = jnp.repeat(jnp.arange(rows_n), GRID_W).astype(jnp.float32)
    col = jnp.tile(jnp.arange(GRID_W), rows_n).astype(jnp.float32)
    axis_dim = head_dim // 2
    inv = ROPE_THETA ** (-jnp.arange(0, axis_dim, 2, dtype=jnp.float32) / axis_dim)
    ang_r = row[:, None] * inv[None]
    ang_c = col[:, None] * inv[None]
    return (jnp.cos(ang_r), jnp.sin(ang_r), jnp.cos(ang_c), jnp.sin(ang_c))


def _rope_1d(x, cos, sin):
    shape = (x.shape[1],) + (1,) * (x.ndim - 3) + (cos.shape[-1],)
    cos = cos.reshape(shape)
    sin = sin.reshape(shape)
    x1, x2 = jnp.split(x.astype(jnp.float32), 2, axis=-1)
    return jnp.concatenate([x1 * cos - x2 * sin, x2 * cos + x1 * sin], axis=-1)


def _rope_2d(x, tabs):
    cr, sr, cc, sc = tabs
    xr, xc = jnp.split(x, 2, axis=-1)
    return jnp.concatenate([_rope_1d(xr, cr, sr), _rope_1d(xc, cc, sc)], axis=-1).astype(x.dtype)


def _sweep_queries(fn, q):
    B, L = q.shape[0], q.shape[1]
    nb = L // Q_BLOCK
    qb = jnp.moveaxis(q.reshape((B, nb, Q_BLOCK) + q.shape[2:]), 1, 0)
    out = jnp.moveaxis(lax.map(fn, qb), 0, 1)
    return out.reshape((B, L) + out.shape[3:])


def _diff_attend(q, k, v, lam):
    scale = DA_QK_DIM ** -0.5
    def body(qb):
        s = jnp.einsum('bqhcd,bkhcd->bhcqk', qb, k).astype(jnp.float32) * scale
        p = jax.nn.softmax(s, axis=-1)
        a = (p[:, :, 0] - lam * p[:, :, 1]).astype(v.dtype)
        return jnp.einsum('bhqk,bkhe->bqhe', a, v)
    return _sweep_queries(body, q)


def _gqa_attend(q, k, v):
    scale = GQ_DIM ** -0.5
    def body(qb):
        s = jnp.einsum('bqgrd,bkgd->bgrqk', qb, k).astype(jnp.float32) * scale
        p = jax.nn.softmax(s, axis=-1).astype(v.dtype)
        return jnp.einsum('bgrqk,bkgd->bqgrd', p, v)
    return _sweep_queries(body, q)


def _hyena_filter(L, w1, b1, w2, b2, w3, b3, w4, freq):
    f32 = jnp.float32
    t = jnp.linspace(0.0, 1.0, L, dtype=f32)[:, None]
    w = 2.0 * math.pi * jnp.arange(L, dtype=f32)[:, None] / L
    f = jnp.linspace(1e-4, HY_BANDS - 1, HY_BANDS, dtype=f32)[None]
    z = jnp.concatenate([t, jnp.cos(f * w), -jnp.sin(f * w)], axis=-1)
    fr = freq.astype(f32)
    h = jnp.sin(fr * (z @ w1.astype(f32) + b1.astype(f32)))
    h = jnp.sin(fr * (h @ w2.astype(f32) + b2.astype(f32)))
    h = jnp.sin(fr * (h @ w3.astype(f32) + b3.astype(f32)))
    h = h @ w4.astype(f32)
    deltas = jnp.linspace(HY_MIN_DECAY, HY_MAX_DECAY, HY_WIDTH, dtype=f32)
    decay = jnp.exp(-t * jnp.abs(deltas)[None])
    h = h.reshape(L, 2, HY_WIDTH) * decay[:, None, :]
    fwd, bwd = h[:, 0], h[:, 1]
    full = jnp.concatenate([fwd, jnp.zeros((1, HY_WIDTH), f32), bwd[:0:-1]], axis=0)
    return full * lax.rsqrt(jnp.sum(full * full, axis=0, keepdims=True) + EPS)


def _long_conv(u, filt, skip):
    L = u.shape[1]
    n = 2 * L
    uf = u.astype(jnp.float32)
    U = jnp.fft.rfft(uf, n=n, axis=1)
    Hf = jnp.fft.rfft(filt, n=n, axis=0)
    y = jnp.fft.irfft(U * Hf[None], n=n, axis=1)[:, :L]
    return (y + uf * skip.astype(jnp.float32)).astype(u.dtype)


def _short_conv(u, w, b):
    C = u.shape[-1]
    y = lax.conv_general_dilated(u, w[:, None, :].astype(u.dtype), window_strides=(1,),
                                 padding=[(1, 1)], dimension_numbers=('NWC', 'WIO', 'NWC'),
                                 feature_group_count=C)
    return y + b


def _hyena(u, filt, short_w, short_b, skip):
    u = _short_conv(u, short_w, short_b)
    x0, x1, v = jnp.split(u, 3, axis=-1)
    return x0 * _long_conv(v * x1, filt, skip)


def _prep(p, q_g, k_g):
    qa, ka, va, ga, qg, kg, vg, gg, uh, gh = jnp.split(p, np.cumsum(COL_SIZES)[:-1].tolist(), axis=-1)
    B, L = p.shape[0], p.shape[1]
    qa = qa.reshape(B, L, DA_HEADS, 2, DA_QK_DIM)
    ka = ka.reshape(B, L, DA_HEADS, 2, DA_QK_DIM)
    va = va.reshape(B, L, DA_HEADS, DA_V_DIM)
    qg = _rms(qg.reshape(B, L, GQ_KV_HEADS, GQ_GROUP, GQ_DIM), q_g)
    kg = _rms(kg.reshape(B, L, GQ_KV_HEADS, GQ_DIM), k_g)
    vg = vg.reshape(B, L, GQ_KV_HEADS, GQ_DIM)
    return qa, ka, va, ga, qg, kg, vg, gg, uh, gh


def _mixer_output(qa, ka, va, ga, qg, kg, vg, gg, uh, gh, filt, lam, lam_init,
                  da_subln_g, gq_out_g, hy_short_w, hy_short_b, hy_bias, hy_out_g, w_out):
    B, L = qa.shape[0], qa.shape[1]
    oa = _rms(_diff_attend(qa, ka, va, lam), da_subln_g) * (1.0 - lam_init)
    oa = oa.reshape(B, L, DA_WIDTH)
    ob = _rms(_gqa_attend(qg, kg, vg).reshape(B, L, GQ_WIDTH), gq_out_g)
    oc = _rms(_hyena(uh, filt, hy_short_w, hy_short_b, hy_bias), hy_out_g)
    y = jnp.concatenate([oa * jax.nn.silu(ga), ob * jax.nn.silu(gg), oc * jax.nn.silu(gh)], axis=-1)
    return y @ w_out


def setup_inputs(seed: int = 0) -> dict:
    key = jax.random.key(seed)
    ks = jax.random.split(key, 27)
    f32 = jnp.float32
    def nrm(k, shape, s):
        return s * jax.random.normal(k, shape, f32)
    def gain(k, shape):
        return 1.0 + 0.02 * jax.random.normal(k, shape, f32)
    D = D_MODEL
    return {
        'x': nrm(ks[0], (BATCH, SEQ, D), 1.0),
        'c': nrm(ks[1], (BATCH, D), 1.0),
        'ctx': nrm(ks[2], (BATCH, CTX_LEN, D), 1.0),
        'c_ctx': nrm(ks[3], (D,), 1.0),
        'ada_w': nrm(ks[4], (DEPTH, D, 3 * D), D ** -0.5),
        'ada_b': nrm(ks[5], (DEPTH, 3 * D), 0.01),
        'norm_g': gain(ks[6], (DEPTH, D)),
        'w_in': nrm(ks[7], (DEPTH, D, D_IN), D ** -0.5),
        'w_out': nrm(ks[8], (DEPTH, D_MIX, D), D_MIX ** -0.5),
        'da_lambda': nrm(ks[9], (DEPTH, 4, DA_QK_DIM), 0.1),
        'da_subln_g': gain(ks[10], (DEPTH, DA_V_DIM)),
        'gq_q_g': gain(ks[11], (DEPTH, GQ_DIM)),
        'gq_k_g': gain(ks[12], (DEPTH, GQ_DIM)),
        'gq_out_g': gain(ks[13], (DEPTH, GQ_WIDTH)),
        'hy_short_w': nrm(ks[14], (DEPTH, HY_SHORT, HY_U), HY_SHORT ** -0.5),
        'hy_short_b': nrm(ks[15], (DEPTH, HY_U), 0.01),
        'hy_w1': nrm(ks[16], (DEPTH, HY_EMB, HY_FFN), HY_EMB ** -0.5),
        'hy_b1': nrm(ks[17], (DEPTH, HY_FFN), 0.1),
        'hy_w2': nrm(ks[18], (DEPTH, HY_FFN, HY_FFN), HY_FFN ** -0.5),
        'hy_b2': nrm(ks[19], (DEPTH, HY_FFN), 0.1),
        'hy_w3': nrm(ks[20], (DEPTH, HY_FFN, HY_FFN), HY_FFN ** -0.5),
        'hy_b3': nrm(ks[21], (DEPTH, HY_FFN), 0.1),
        'hy_w4': nrm(ks[22], (DEPTH, HY_FFN, 2 * HY_WIDTH), HY_FFN ** -0.5),
        'hy_freq': gain(ks[23], (DEPTH, HY_FFN)),
        'hy_bias': nrm(ks[24], (DEPTH, HY_WIDTH), 1.0),
        'hy_out_g': gain(ks[25], (DEPTH, HY_WIDTH)),
        'final_g': gain(ks[26], (D,)),
    }


def reference(x, c, ctx, c_ctx, ada_w, ada_b, norm_g, w_in, w_out, da_lambda, da_subln_g,
              gq_q_g, gq_k_g, gq_out_g, hy_short_w, hy_short_b, hy_w1, hy_b1, hy_w2, hy_b2,
              hy_w3, hy_b3, hy_w4, hy_freq, hy_bias, hy_out_g, final_g):
    L = x.shape[1]
    Lc = ctx.shape[1]
    tab_da = _axial_tables(L, DA_QK_DIM)
    tab_gq = _axial_tables(L, GQ_DIM)
    xc = ctx
    for l in range(DEPTH):
        update_ctx = l < DEPTH - 1
        lam_init = 0.8 - 0.6 * math.exp(-0.3 * l)
        lmb = da_lambda[l].astype(jnp.float32)
        lam = jnp.exp(jnp.sum(lmb[0] * lmb[1])) - jnp.exp(jnp.sum(lmb[2] * lmb[3])) + lam_init
        shift, scale, gate = jnp.split(jax.nn.silu(c) @ ada_w[l] + ada_b[l], 3, axis=-1)
        shift_c, scale_c, gate_c = jnp.split(jax.nn.silu(c_ctx) @ ada_w[l] + ada_b[l], 3, axis=-1)
        h = _rms(x, norm_g[l]) * (1.0 + scale[:, None]) + shift[:, None]
        hc = _rms(xc, norm_g[l]) * (1.0 + scale_c) + shift_c
        qa, ka, va, ga, qg, kg, vg, gg, uh, gh = _prep(h @ w_in[l], gq_q_g[l], gq_k_g[l])
        qa_c, ka_c, va_c, ga_c, qg_c, kg_c, vg_c, gg_c, uh_c, gh_c = _prep(hc @ w_in[l], gq_q_g[l], gq_k_g[l])
        ka_all = jnp.concatenate([ka_c, _rope_2d(ka, tab_da)], axis=1)
        va_all = jnp.concatenate([va_c, va], axis=1)
        kg_all = jnp.concatenate([kg_c, _rope_2d(kg, tab_gq)], axis=1)
        vg_all = jnp.concatenate([vg_c, vg], axis=1)
        filt = _hyena_filter(L, hy_w1[l], hy_b1[l], hy_w2[l], hy_b2[l], hy_w3[l], hy_b3[l], hy_w4[l], hy_freq[l])
        y = _mixer_output(_rope_2d(qa, tab_da), ka_all, va_all, ga, _rope_2d(qg, tab_gq), kg_all, vg_all, gg,
                          uh, gh, filt, lam, lam_init, da_subln_g[l], gq_out_g[l],
                          hy_short_w[l], hy_short_b[l], hy_bias[l], hy_out_g[l], w_out[l])
        if update_ctx:
            filt_c = _hyena_filter(Lc, hy_w1[l], hy_b1[l], hy_w2[l], hy_b2[l], hy_w3[l], hy_b3[l], hy_w4[l], hy_freq[l])
            yc = _mixer_output(qa_c, ka_c, va_c, ga_c, qg_c, kg_c, vg_c, gg_c, uh_c, gh_c, filt_c, lam, lam_init,
                               da_subln_g[l], gq_out_g[l], hy_short_w[l], hy_short_b[l], hy_bias[l],
                               hy_out_g[l], w_out[l])
            xc = xc + gate_c * yc
        x = x + gate[:, None] * y
    return _rms(x, final_g)
```

```python
import functools
import math

import jax
import jax.numpy as jnp
import numpy as np
from jax import lax
from jax.experimental import pallas as pl
from jax.experimental.pallas import tpu as pltpu

F32 = jnp.float32
BF16 = jnp.bfloat16

GRID_W = 64
EPS = 1e-6
ROPE_THETA = 10000.0
DA_HEADS = 4
DA_QK_DIM = 64
DA_V_DIM = 2 * DA_QK_DIM
DA_WIDTH = DA_HEADS * DA_V_DIM
GQ_HEADS = 8
GQ_KV_HEADS = 2
GQ_GROUP = GQ_HEADS // GQ_KV_HEADS
GQ_DIM = 128
GQ_WIDTH = GQ_HEADS * GQ_DIM
HY_WIDTH = 512
HY_EMB = 33
HY_BANDS = (HY_EMB - 1) // 2
HY_FFN = 64
HY_MAX_DECAY = math.log(1e-2) / 0.3
HY_MIN_DECAY = math.log(1e-2) / 1.5
HY_U = 3 * HY_WIDTH
D_MIX = DA_WIDTH + GQ_WIDTH + HY_WIDTH
COL_SIZES = (DA_WIDTH, DA_WIDTH, DA_WIDTH, DA_WIDTH,
             GQ_WIDTH, GQ_KV_HEADS * GQ_DIM, GQ_KV_HEADS * GQ_DIM, GQ_WIDTH,
             HY_U, HY_WIDTH)
COL_OFFS = tuple(int(v) for v in np.cumsum((0,) + COL_SIZES))
D_IN = COL_OFFS[-1]

LANE = 128
VMEM_LIMIT = 56 * 1024 * 1024


def _cparams(*sem):
    return pltpu.CompilerParams(dimension_semantics=sem, vmem_limit_bytes=VMEM_LIMIT)


def _silu(g):
    return g / (1.0 + jnp.exp(-g))


def _rms_rows(x, g):
    return x * lax.rsqrt(jnp.mean(x * x, axis=-1, keepdims=True) + EPS) * g


def _mod_kernel(c_ref, w_ref, b_ref, o_ref):
    c = c_ref[...]
    o_ref[...] = jnp.dot(_silu(c), w_ref[...], preferred_element_type=F32,
                         precision=lax.Precision.HIGHEST) + b_ref[...]


def _modulation(c_all, ada_w, ada_b):
    depth, d, n3 = ada_w.shape
    r = c_all.shape[0]
    tn = 512
    return pl.pallas_call(
        _mod_kernel,
        grid=(depth, n3 // tn),
        in_specs=[pl.BlockSpec((r, d), lambda l, j: (0, 0)),
                  pl.BlockSpec((None, d, tn), lambda l, j: (l, 0, j)),
                  pl.BlockSpec((None, 1, tn), lambda l, j: (l, 0, j))],
        out_specs=pl.BlockSpec((None, r, tn), lambda l, j: (l, 0, j)),
        out_shape=jax.ShapeDtypeStruct((depth, r, n3), F32),
        compiler_params=_cparams("parallel", "parallel"),
        name="modulation",
    )(c_all, ada_w, ada_b.reshape(depth, 1, n3))


def _rope_tables(L, head_dim, reps):
    rows_n = L // GRID_W
    row = jnp.repeat(jnp.arange(rows_n), GRID_W).astype(F32)
    col = jnp.tile(jnp.arange(GRID_W), rows_n).astype(F32)
    axis_dim = head_dim // 2
    inv = ROPE_THETA ** (-jnp.arange(0, axis_dim, 2, dtype=F32) / axis_dim)
    ar, ac = row[:, None] * inv[None], col[:, None] * inv[None]
    z = jnp.zeros_like(ar)
    cos = jnp.concatenate([jnp.cos(ar), jnp.cos(ar), jnp.cos(ac), jnp.cos(ac)], axis=-1)
    s_lo = jnp.concatenate([-jnp.sin(ar), z, -jnp.sin(ac), z], axis=-1)
    s_hi = jnp.concatenate([z, jnp.sin(ar), z, jnp.sin(ac)], axis=-1)
    return tuple(jnp.tile(t, (1, reps)) for t in (cos, s_lo, s_hi))


def _rope_block(x, cos, s_lo, s_hi, half):
    fwd = pltpu.roll(x, LANE - half, axis=1)
    bwd = pltpu.roll(x, half, axis=1)
    return x * cos + fwd * s_lo + bwd * s_hi


def _inproj_kernel(*refs, rope):
    x_ref, sc_ref, sh_ref, ng_ref, w_ref, qg_g_ref, kg_g_ref = refs[:7]
    if rope:
        tabs = refs[7:13]
        outs = refs[13:]
    else:
        tabs = None
        outs = refs[7:]
    qa_o, ka_o, va_o, ga_o, qg_o, kg_o, vg_o, gg_o, uh_o, gh_o = outs

    x = x_ref[...]
    h = _rms_rows(x, ng_ref[...])
    h = h * (1.0 + sc_ref[...]) + sh_ref[...]
    hb = h.astype(BF16)

    def proj(group, blk):
        off = COL_OFFS[group] + blk * LANE
        return jnp.dot(hb, w_ref[:, off:off + LANE], preferred_element_type=F32)

    def rope_da(v):
        if not rope:
            return v
        return _rope_block(v, tabs[0][...], tabs[1][...], tabs[2][...], DA_QK_DIM // 4)

    def rope_gq(v):
        if not rope:
            return v
        return _rope_block(v, tabs[3][...], tabs[4][...], tabs[5][...], GQ_DIM // 4)

    for b in range(DA_WIDTH // LANE):
        sl = slice(b * LANE, (b + 1) * LANE)
        qa_o[:, sl] = (rope_da(proj(0, b)) * (DA_QK_DIM ** -0.5)).astype(qa_o.dtype)
        ka_o[:, sl] = rope_da(proj(1, b)).astype(ka_o.dtype)
        va_o[:, sl] = proj(2, b).astype(va_o.dtype)
        ga_o[:, sl] = _silu(proj(3, b)).astype(ga_o.dtype)
        gh_o[:, sl] = _silu(proj(9, b)).astype(gh_o.dtype)
    for b in range(GQ_WIDTH // LANE):
        sl = slice(b * LANE, (b + 1) * LANE)
        q = _rms_rows(proj(4, b), qg_g_ref[...])
        qg_o[:, sl] = (rope_gq(q) * (GQ_DIM ** -0.5)).astype(qg_o.dtype)
        gg_o[:, sl] = _silu(proj(7, b)).astype(gg_o.dtype)
    for b in range(GQ_KV_HEADS):
        sl = slice(b * LANE, (b + 1) * LANE)
        k = _rms_rows(proj(5, b), kg_g_ref[...])
        kg_o[:, sl] = rope_gq(k).astype(kg_o.dtype)
        vg_o[:, sl] = proj(6, b).astype(vg_o.dtype)
    for b in range(HY_U // LANE):
        sl = slice(b * LANE, (b + 1) * LANE)
        uh_o[:, sl] = proj(8, b)


def _inproj(xt, scale, shift, norm_g, w_bf, q_g, k_g, tabs):
    B, T, D = xt.shape
    tm = min(256, T)
    rope = tabs is not None
    row = lambda b, i: (b, i, 0)
    const2 = lambda b, i: (0, 0)
    in_specs = [pl.BlockSpec((None, tm, D), row),
                pl.BlockSpec((None, 1, D), lambda b, i: (b, 0, 0)),
                pl.BlockSpec((None, 1, D), lambda b, i: (b, 0, 0)),
                pl.BlockSpec((1, D), const2),
                pl.BlockSpec((D, D_IN), const2),
                pl.BlockSpec((1, GQ_DIM), const2),
                pl.BlockSpec((1, GQ_DIM), const2)]
    args = [xt, scale, shift, norm_g.reshape(1, D), w_bf, q_g.reshape(1, GQ_DIM), k_g.reshape(1, GQ_DIM)]
    if rope:
        in_specs += [pl.BlockSpec((tm, LANE), lambda b, i: (i, 0))] * 6
        args += list(tabs)
    widths = (DA_WIDTH, DA_WIDTH, DA_WIDTH, DA_WIDTH, GQ_WIDTH, GQ_KV_HEADS * GQ_DIM,
              GQ_KV_HEADS * GQ_DIM, GQ_WIDTH, HY_U, HY_WIDTH)
    dtypes = (BF16, BF16, BF16, BF16, BF16, BF16, BF16, BF16, F32, BF16)
    return pl.pallas_call(
        functools.partial(_inproj_kernel, rope=rope),
        grid=(B, T // tm),
        in_specs=in_specs,
        out_specs=[pl.BlockSpec((None, tm, w), row) for w in widths],
        out_shape=[jax.ShapeDtypeStruct((B, T, w), dt) for w, dt in zip(widths, dtypes)],
        compiler_params=_cparams("parallel", "parallel"),
        name="inproj_rope" if rope else "inproj_ctx",
    )(*args)


def _softmax_pv(qs, k, v):
    s = lax.dot_general(qs, k, (((1,), (1,)), ((), ())), preferred_element_type=F32)
    m = jnp.max(s, axis=-1, keepdims=True)
    p = jnp.exp(s - m)
    l = jnp.sum(p, axis=-1, keepdims=True)
    pv = jnp.dot(p.astype(BF16), v, preferred_element_type=F32)
    return pv / l


def _da_kernel(lam_ref, q_ref, k_ref, v_ref, gate_ref, sg_ref, o_ref, *, lam_init):
    tq = q_ref.shape[0]
    q = q_ref[...]
    lane = lax.broadcasted_iota(jnp.int32, q.shape, 1)
    zero = jnp.zeros_like(q)
    qs = jnp.concatenate([jnp.where(lane < DA_QK_DIM, q, zero),
                          jnp.where(lane >= DA_QK_DIM, q, zero)], axis=0)
    on = _softmax_pv(qs, k_ref[...], v_ref[...])
    lm = lam_ref[...]
    lam = (jnp.exp(jnp.sum(lm[0:1] * lm[1:2], axis=-1, keepdims=True))
           - jnp.exp(jnp.sum(lm[2:3] * lm[3:4], axis=-1, keepdims=True)) + lam_init)
    o = on[:tq] - lam * on[tq:]
    o = _rms_rows(o, sg_ref[...]) * (1.0 - lam_init)
    o_ref[...] = (o * gate_ref[...].astype(F32)).astype(o_ref.dtype)


def _da_attention(lam_p, q, k, v, gate, subln_g, lam_init):
    B, Tq, _ = q.shape
    Tk = k.shape[1]
    tq = min(256, Tq)
    qmap = lambda b, h, i: (b, i, h)
    kmap = lambda b, h, i: (b, 0, h)
    return pl.pallas_call(
        functools.partial(_da_kernel, lam_init=lam_init),
        grid=(B, DA_HEADS, Tq // tq),
        in_specs=[pl.BlockSpec((4, DA_QK_DIM), lambda b, h, i: (0, 0)),
                  pl.BlockSpec((None, tq, LANE), qmap),
                  pl.BlockSpec((None, Tk, LANE), kmap),
                  pl.BlockSpec((None, Tk, LANE), kmap),
                  pl.BlockSpec((None, tq, LANE), qmap),
                  pl.BlockSpec((1, LANE), lambda b, h, i: (0, 0))],
        out_specs=pl.BlockSpec((None, tq, LANE), qmap),
        out_shape=jax.ShapeDtypeStruct((B, Tq, DA_WIDTH), BF16),
        compiler_params=_cparams("parallel", "parallel", "arbitrary"),
        name="diff_attention",
    )(lam_p, q, k, v, gate, subln_g.reshape(1, DA_V_DIM))


def _gq_kernel(q_ref, k_ref, v_ref, o_ref):
    tq = q_ref.shape[0]
    qs = jnp.concatenate([q_ref[:, r * LANE:(r + 1) * LANE] for r in range(GQ_GROUP)], axis=0)
    on = _softmax_pv(qs, k_ref[...], v_ref[...])
    for r in range(GQ_GROUP):
        o_ref[:, r * LANE:(r + 1) * LANE] = on[r * tq:(r + 1) * tq].astype(o_ref.dtype)


def _gq_attention(q, k, v):
    B, Tq, _ = q.shape
    Tk = k.shape[1]
    tq = min(128, Tq)
    gw = GQ_GROUP * GQ_DIM
    qmap = lambda b, g, i: (b, i, g)
    kmap = lambda b, g, i: (b, 0, g)
    return pl.pallas_call(
        _gq_kernel,
        grid=(B, GQ_KV_HEADS, Tq // tq),
        in_specs=[pl.BlockSpec((None, tq, gw), qmap),
                  pl.BlockSpec((None, Tk, LANE), kmap),
                  pl.BlockSpec((None, Tk, LANE), kmap)],
        out_specs=pl.BlockSpec((None, tq, gw), qmap),
        out_shape=jax.ShapeDtypeStruct((B, Tq, GQ_WIDTH), F32),
        compiler_params=_cparams("parallel", "parallel", "arbitrary"),
        name="gqa_attention",
    )(q, k, v)


def _hy_pre_kernel(u0_ref, u1_ref, u2_ref, w0_ref, w1_ref, w2_ref, b0_ref, b1_ref, b2_ref,
                   x0_ref, z_ref, zb_ref):
    T = u0_ref.shape[0]
    row = lax.broadcasted_iota(jnp.int32, u0_ref.shape, 0)

    def sconv(u_ref, w_ref, b_ref):
        u = u_ref[...]
        w = w_ref[...]
        prev = jnp.where(row == 0, 0.0, pltpu.roll(u, 1, axis=0))
        nxt = jnp.where(row == T - 1, 0.0, pltpu.roll(u, T - 1, axis=0))
        return prev * w[0:1] + u * w[1:2] + nxt * w[2:3] + b_ref[...]

    x0_ref[...] = sconv(u0_ref, w0_ref, b0_ref)
    z = sconv(u2_ref, w2_ref, b2_ref) * sconv(u1_ref, w1_ref, b1_ref)
    z_ref[...] = z
    zb_ref[...] = z.astype(BF16)


def _hy_pre(uh, short_w, short_b):
    B, T, _ = uh.shape
    nb = HY_WIDTH // LANE
    u_specs = [pl.BlockSpec((None, T, LANE), lambda b, j, p=p: (b, 0, p * nb + j)) for p in range(3)]
    w_specs = [pl.BlockSpec((3, LANE), lambda b, j, p=p: (0, p * nb + j)) for p in range(3)]
    b_specs = [pl.BlockSpec((1, LANE), lambda b, j, p=p: (0, p * nb + j)) for p in range(3)]
    out_spec = pl.BlockSpec((None, T, LANE), lambda b, j: (b, 0, j))
    sb = short_b.reshape(1, HY_U)
    return pl.pallas_call(
        _hy_pre_kernel,
        grid=(B, nb),
        in_specs=u_specs + w_specs + b_specs,
        out_specs=[out_spec] * 3,
        out_shape=[jax.ShapeDtypeStruct((B, T, HY_WIDTH), F32),
                   jax.ShapeDtypeStruct((B, T, HY_WIDTH), F32),
                   jax.ShapeDtypeStruct((B, T, HY_WIDTH), BF16)],
        compiler_params=_cparams("parallel", "parallel"),
        name="hyena_short_conv",
    )(uh, uh, uh, short_w, short_w, short_w, sb, sb, sb)


def _filter_kernel(z_ref, w1_ref, b1_ref, w2_ref, b2_ref, w3_ref, b3_ref, w4f_ref, w4b_ref,
                   fr_ref, t_ref, dl_ref, fe_ref, fo_ref):
    hp = lax.Precision.HIGHEST
    fr = fr_ref[...]
    h = jnp.sin(fr * (jnp.dot(z_ref[...], w1_ref[...], preferred_element_type=F32, precision=hp) + b1_ref[...]))
    h = jnp.sin(fr * (jnp.dot(h, w2_ref[...], preferred_element_type=F32, precision=hp) + b2_ref[...]))
    h = jnp.sin(fr * (jnp.dot(h, w3_ref[...], preferred_element_type=F32, precision=hp) + b3_ref[...]))
    decay = jnp.exp(-t_ref[...] * jnp.abs(dl_ref[...]))
    fwd = jnp.dot(h, w4f_ref[...], preferred_element_type=F32, precision=hp) * decay
    bwd = jnp.dot(h, w4b_ref[...], preferred_element_type=F32, precision=hp) * decay
    row = lax.broadcasted_iota(jnp.int32, bwd.shape, 0)
    bwd = jnp.where(row == 0, 0.0, bwd)
    ss = jnp.sum(fwd * fwd + bwd * bwd, axis=0, keepdims=True)
    sc = lax.rsqrt(ss + EPS)
    fe_ref[...] = ((fwd + bwd) * sc).astype(fe_ref.dtype)
    fo_ref[...] = ((fwd - bwd) * sc).astype(fo_ref.dtype)


def _hyena_filter_eo(T, w1, b1, w2, b2, w3, b3, w4, freq):
    t = jnp.linspace(0.0, 1.0, T, dtype=F32)[:, None]
    w = 2.0 * math.pi * jnp.arange(T, dtype=F32)[:, None] / T
    f = jnp.linspace(1e-4, HY_BANDS - 1, HY_BANDS, dtype=F32)[None]
    z = jnp.concatenate([t, jnp.cos(f * w), -jnp.sin(f * w)], axis=-1)
    z = jnp.pad(z, ((0, 0), (0, LANE - HY_EMB)))
    w1p = jnp.pad(w1, ((0, LANE - HY_EMB), (0, 0)))
    deltas = jnp.linspace(HY_MIN_DECAY, HY_MAX_DECAY, HY_WIDTH, dtype=F32)[None]
    nb = HY_WIDTH // LANE
    c2 = lambda j: (0, 0)
    vec = lambda a: a.reshape(1, HY_FFN)
    return pl.pallas_call(
        _filter_kernel,
        grid=(nb,),
        in_specs=[pl.BlockSpec((T, LANE), c2),
                  pl.BlockSpec((LANE, HY_FFN), c2), pl.BlockSpec((1, HY_FFN), c2),
                  pl.BlockSpec((HY_FFN, HY_FFN), c2), pl.BlockSpec((1, HY_FFN), c2),
                  pl.BlockSpec((HY_FFN, HY_FFN), c2), pl.BlockSpec((1, HY_FFN), c2),
                  pl.BlockSpec((HY_FFN, LANE), lambda j: (0, j)),
                  pl.BlockSpec((HY_FFN, LANE), lambda j: (0, nb + j)),
                  pl.BlockSpec((1, HY_FFN), c2),
                  pl.BlockSpec((T, 1), c2),
                  pl.BlockSpec((1, LANE), lambda j: (0, j))],
        out_specs=[pl.BlockSpec((T, LANE), lambda j: (0, j))] * 2,
        out_shape=[jax.ShapeDtypeStruct((T, HY_WIDTH), BF16)] * 2,
        compiler_params=_cparams("parallel"),
        name="hyena_filter",
    )(z, w1p, vec(b1), w2, vec(b2), w3, vec(b3), w4, w4, vec(freq), t, deltas)


def _dft_tile_rows(T):
    return min(1024, 2 * T)


def _dft_tables(T):
    n = 2 * T
    tr = _dft_tile_rows(T)
    half = tr // 2
    r = np.arange(n)
    local = r % tr
    k = (r // tr) * half + local % half
    is_sin = local >= half
    nyq = is_sin & (k == 0)
    k = np.where(nyq, T, k)
    is_sin = is_sin & ~nyq
    lo = 64 if T % 64 == 0 else 1
    hi = T // lo
    ang = 2.0 * np.pi / n
    a_lo = ((k[:, None] * np.arange(lo)[None]) % n) * ang
    a_hi = ((k[:, None] * (np.arange(hi)[None] * lo)) % n) * ang
    sel = is_sin[:, None]
    x_lo = jnp.asarray(np.where(sel, np.sin(a_lo), np.cos(a_lo)), F32)
    w_lo = jnp.asarray(np.where(sel, np.cos(a_lo), -np.sin(a_lo)), F32)
    c_hi = jnp.asarray(np.cos(a_hi), F32)
    s_hi = jnp.asarray(np.sin(a_hi), F32)
    a = (x_lo[:, None, :] * c_hi[:, :, None] + w_lo[:, None, :] * s_hi[:, :, None]).reshape(n, T)
    at = (x_lo.T[None, :, :] * c_hi.T[:, None, :] + w_lo.T[None, :, :] * s_hi.T[:, None, :]).reshape(T, n)
    return a.astype(BF16), at.astype(BF16)


def _dft_raw_kernel(a_ref, z_ref, o_ref):
    o_ref[...] = jnp.dot(a_ref[...], z_ref[...], preferred_element_type=F32)


def _dft_raw(a, f):
    n, T = a.shape
    N = f.shape[1]
    tr = _dft_tile_rows(T)
    return pl.pallas_call(
        _dft_raw_kernel,
        grid=(n // tr,),
        in_specs=[pl.BlockSpec((tr, T), lambda i: (i, 0)),
                  pl.BlockSpec((T, N), lambda i: (0, 0))],
        out_specs=pl.BlockSpec((tr, N), lambda i: (i, 0)),
        out_shape=jax.ShapeDtypeStruct((n, N), F32),
        compiler_params=_cparams("parallel"),
        name="dft_filter",
    )(a, f)


def _filter_spectrum(a, fe, fo):
    n, T = a.shape
    tr = _dft_tile_rows(T)
    o = _dft_raw(a, jnp.concatenate([fe, fo], axis=1))
    o = o.reshape(n // tr, 2, tr // 2, 2 * HY_WIDTH)
    cosp = o[:, 0].reshape(T, 2 * HY_WIDTH)
    sinp = o[:, 1].reshape(T, 2 * HY_WIDTH)
    hr = cosp[:, :HY_WIDTH]
    hi = -sinp[:, HY_WIDTH:]
    hnyq = sinp[0:1, :HY_WIDTH]
    first = (jnp.arange(T) == 0)[:, None]
    ck = jnp.where(first, 1.0 / n, 2.0 / n)
    t1 = ck * hr
    t2 = jnp.where(first, 0.0, ck * hi)
    t4 = jnp.where(first, hnyq / n, ck * hr)
    return t1, t2, t4


def _dft_fwd_kernel(a_ref, z_ref, t1_ref, t2_ref, t4_ref, y_ref):
    half = a_ref.shape[0] // 2
    u = jnp.dot(a_ref[...], z_ref[...], preferred_element_type=F32)
    re, im = u[:half], u[half:]
    t2 = t2_ref[...]
    y_ref[:half, :] = (re * t1_ref[...] + im * t2).astype(y_ref.dtype)
    y_ref[half:, :] = (im * t4_ref[...] - re * t2).astype(y_ref.dtype)


def _dft_fwd(a, zb, t1, t2, t4):
    n, T = a.shape
    B = zb.shape[0]
    tr = _dft_tile_rows(T)
    tmap = lambda i, b: (i, 0)
    return pl.pallas_call(
        _dft_fwd_kernel,
        grid=(n // tr, B),
        in_specs=[pl.BlockSpec((tr, T), tmap),
                  pl.BlockSpec((None, T, HY_WIDTH), lambda i, b: (b, 0, 0)),
                  pl.BlockSpec((tr // 2, HY_WIDTH), tmap),
                  pl.BlockSpec((tr // 2, HY_WIDTH), tmap),
                  pl.BlockSpec((tr // 2, HY_WIDTH), tmap)],
        out_specs=pl.BlockSpec((None, tr, HY_WIDTH), lambda i, b: (b, i, 0)),
        out_shape=jax.ShapeDtypeStruct((B, n, HY_WIDTH), BF16),
        compiler_params=_cparams("parallel", "arbitrary"),
        name="dft_forward",
    )(a, zb, t1, t2, t4)


def _dft_inv_kernel(at_ref, y_ref, x0_ref, z_ref, skip_ref, o_ref):
    y = jnp.dot(at_ref[...], y_ref[...], preferred_element_type=F32)
    o_ref[...] = x0_ref[...] * (y + z_ref[...] * skip_ref[...])


def _dft_inv(at, y, x0, z, skip):
    T, n = at.shape
    B = y.shape[0]
    tt = min(512, T)
    rmap = lambda i, b: (b, i, 0)
    return pl.pallas_call(
        _dft_inv_kernel,
        grid=(T // tt, B),
        in_specs=[pl.BlockSpec((tt, n), lambda i, b: (i, 0)),
                  pl.BlockSpec((None, n, HY_WIDTH), lambda i, b: (b, 0, 0)),
                  pl.BlockSpec((None, tt, HY_WIDTH), rmap),
                  pl.BlockSpec((None, tt, HY_WIDTH), rmap),
                  pl.BlockSpec((1, HY_WIDTH), lambda i, b: (0, 0))],
        out_specs=pl.BlockSpec((None, tt, HY_WIDTH), rmap),
        out_shape=jax.ShapeDtypeStruct((B, T, HY_WIDTH), F32),
        compiler_params=_cparams("parallel", "arbitrary"),
        name="dft_inverse",
    )(at, y, x0, z, skip.reshape(1, HY_WIDTH))


def _hyena(uh, dft, spectrum, short_w, short_b, skip):
    a, at = dft
    x0, z, zb = _hy_pre(uh, short_w, short_b)
    y = _dft_fwd(a, zb, *spectrum)
    return _dft_inv(at, y, x0, z, skip)


def _outproj_kernel(ya_ref, ogq_ref, gg_ref, ohy_ref, gh_ref, x_ref, gate_ref, w_ref,
                    gqg_ref, hyg_ref, fg_ref, o_ref, *, final):
    yg = _rms_rows(ogq_ref[...], gqg_ref[...]) * gg_ref[...].astype(F32)
    yh = _rms_rows(ohy_ref[...], hyg_ref[...]) * gh_ref[...].astype(F32)
    a0, a1, a2 = DA_WIDTH, DA_WIDTH + GQ_WIDTH, D_MIX
    y = jnp.dot(ya_ref[...], w_ref[0:a0, :], preferred_element_type=F32)
    y = y + jnp.dot(yg.astype(BF16), w_ref[a0:a1, :], preferred_element_type=F32)
    y = y + jnp.dot(yh.astype(BF16), w_ref[a1:a2, :], preferred_element_type=F32)
    out = x_ref[...] + gate_ref[...] * y
    if final:
        out = _rms_rows(out, fg_ref[...])
    o_ref[...] = out


def _outproj(ya, ogq, gg, ohy, gh, xt, gate, w_bf, gq_out_g, hy_out_g, final_g, final):
    B, T, D = xt.shape
    tm = min(512, T)
    row = lambda b, i: (b, i, 0)
    c2 = lambda b, i: (0, 0)
    return pl.pallas_call(
        functools.partial(_outproj_kernel, final=final),
        grid=(B, T // tm),
        in_specs=[pl.BlockSpec((None, tm, DA_WIDTH), row),
                  pl.BlockSpec((None, tm, GQ_WIDTH), row),
                  pl.BlockSpec((None, tm, GQ_WIDTH), row),
                  pl.BlockSpec((None, tm, HY_WIDTH), row),
                  pl.BlockSpec((None, tm, HY_WIDTH), row),
                  pl.BlockSpec((None, tm, D), row),
                  pl.BlockSpec((None, 1, D), lambda b, i: (b, 0, 0)),
                  pl.BlockSpec((D_MIX, D), c2),
                  pl.BlockSpec((1, GQ_WIDTH), c2),
                  pl.BlockSpec((1, HY_WIDTH), c2),
                  pl.BlockSpec((1, D), c2)],
        out_specs=pl.BlockSpec((None, tm, D), row),
        out_shape=jax.ShapeDtypeStruct((B, T, D), F32),
        compiler_params=_cparams("parallel", "parallel"),
        name="outproj_final" if final else "outproj",
    )(ya, ogq, gg, ohy, gh, xt, gate, w_bf, gq_out_g.reshape(1, GQ_WIDTH),
      hy_out_g.reshape(1, HY_WIDTH), final_g.reshape(1, D))


def kernel(x, c, ctx, c_ctx, ada_w, ada_b, norm_g, w_in, w_out, da_lambda, da_subln_g, gq_q_g, gq_k_g, gq_out_g, hy_short_w, hy_short_b, hy_w1, hy_b1, hy_w2, hy_b2, hy_w3, hy_b3, hy_w4, hy_freq, hy_bias, hy_out_g, final_g):
    B, L, D = x.shape
    Lc = ctx.shape[1]
    depth = ada_w.shape[0]
    assert L % GRID_W == 0 and D_IN == w_in.shape[2]

    tabs = _rope_tables(L, DA_QK_DIM, 2) + _rope_tables(L, GQ_DIM, 1)
    dft_lat = _dft_tables(L)
    dft_ctx = _dft_tables(Lc)

    n_mod = B + 1
    pad = (-n_mod) % 8
    c_all = jnp.concatenate([c, c_ctx[None], jnp.zeros((pad, D), F32)], axis=0)
    mod = _modulation(c_all, ada_w, ada_b)

    w_in_bf = w_in.astype(BF16)
    w_out_bf = w_out.astype(BF16)

    xc = ctx
    for l in range(depth):
        update_ctx = l < depth - 1
        lam_init = 0.8 - 0.6 * math.exp(-0.3 * l)
        shift, scale, gate = (mod[l, :B, i * D:(i + 1) * D][:, None, :] for i in range(3))
        shift_c, scale_c, gate_c = (jnp.broadcast_to(mod[l, B:B + 1, i * D:(i + 1) * D][:, None, :], (B, 1, D))
                                    for i in range(3))

        qa, ka, va, ga, qg, kg, vg, gg, uh, gh = _inproj(
            x, scale, shift, norm_g[l], w_in_bf[l], gq_q_g[l], gq_k_g[l], tabs)
        qa_c, ka_c, va_c, ga_c, qg_c, kg_c, vg_c, gg_c, uh_c, gh_c = _inproj(
            xc, scale_c, shift_c, norm_g[l], w_in_bf[l], gq_q_g[l], gq_k_g[l], None)

        cat = lambda a_c, a: jnp.concatenate([a_c, a], axis=1)
        ya = _da_attention(da_lambda[l], qa, cat(ka_c, ka), cat(va_c, va), ga, da_subln_g[l], lam_init)
        ogq = _gq_attention(qg, cat(kg_c, kg), cat(vg_c, vg))

        filt = (hy_w1[l], hy_b1[l], hy_w2[l], hy_b2[l], hy_w3[l], hy_b3[l], hy_w4[l], hy_freq[l])
        spec = _filter_spectrum(dft_lat[0], *_hyena_filter_eo(L, *filt))
        ohy = _hyena(uh, dft_lat, spec, hy_short_w[l], hy_short_b[l], hy_bias[l])

        if update_ctx:
            ya_c = _da_attention(da_lambda[l], qa_c, ka_c, va_c, ga_c, da_subln_g[l], lam_init)
            ogq_c = _gq_attention(qg_c, kg_c, vg_c)
            spec_c = _filter_spectrum(dft_ctx[0], *_hyena_filter_eo(Lc, *filt))
            ohy_c = _hyena(uh_c, dft_ctx, spec_c, hy_short_w[l], hy_short_b[l], hy_bias[l])
            xc = _outproj(ya_c, ogq_c, gg_c, ohy_c, gh_c, xc, gate_c, w_out_bf[l],
                          gq_out_g[l], hy_out_g[l], final_g, False)

        x = _outproj(ya, ogq, gg, ohy, gh, x, gate, w_out_bf[l],
                     gq_out_g[l], hy_out_g[l], final_g, l == depth - 1)
    return x
```

```python
import functools
import math

import jax
import jax.numpy as jnp
import numpy as np
from jax import lax
from jax.experimental import pallas as pl
from jax.experimental.pallas import tpu as pltpu

F32 = jnp.float32
BF16 = jnp.bfloat16

GRID_W = 64
EPS = 1e-6
ROPE_THETA = 10000.0
DA_HEADS = 4
DA_QK_DIM = 64
DA_V_DIM = 2 * DA_QK_DIM
DA_WIDTH = DA_HEADS * DA_V_DIM
GQ_HEADS = 8
GQ_KV_HEADS = 2
GQ_GROUP = GQ_HEADS // GQ_KV_HEADS
GQ_DIM = 128
GQ_WIDTH = GQ_HEADS * GQ_DIM
HY_WIDTH = 512
HY_EMB = 33
HY_BANDS = (HY_EMB - 1) // 2
HY_FFN = 64
HY_MAX_DECAY = math.log(1e-2) / 0.3
HY_MIN_DECAY = math.log(1e-2) / 1.5
HY_U = 3 * HY_WIDTH
D_MIX = DA_WIDTH + GQ_WIDTH + HY_WIDTH
COL_SIZES = (DA_WIDTH, DA_WIDTH, DA_WIDTH, DA_WIDTH,
             GQ_WIDTH, GQ_KV_HEADS * GQ_DIM, GQ_KV_HEADS * GQ_DIM, GQ_WIDTH,
             HY_U, HY_WIDTH)
COL_OFFS = tuple(int(v) for v in np.cumsum((0,) + COL_SIZES))
D_IN = COL_OFFS[-1]

LANE = 128
DA_CHAIN_ROWS = 256
VMEM_LIMIT = 56 * 1024 * 1024


def _cparams(*sem):
    return pltpu.CompilerParams(dimension_semantics=sem, vmem_limit_bytes=VMEM_LIMIT)


def _silu(g):
    return g / (1.0 + jnp.exp(-g))


def _rms_rows(x, g):
    return x * lax.rsqrt(jnp.mean(x * x, axis=-1, keepdims=True) + EPS) * g


def _mod_kernel(c_ref, w_ref, b_ref, o_ref):
    c = c_ref[...]
    o_ref[...] = jnp.dot(_silu(c), w_ref[...], preferred_element_type=F32,
                         precision=lax.Precision.HIGHEST) + b_ref[...]


def _modulation(c_all, ada_w, ada_b):
    depth, d, n3 = ada_w.shape
    r = c_all.shape[0]
    tn = 512
    return pl.pallas_call(
        _mod_kernel,
        grid=(depth, n3 // tn),
        in_specs=[pl.BlockSpec((r, d), lambda l, j: (0, 0)),
                  pl.BlockSpec((None, d, tn), lambda l, j: (l, 0, j)),
                  pl.BlockSpec((None, 1, tn), lambda l, j: (l, 0, j))],
        out_specs=pl.BlockSpec((None, r, tn), lambda l, j: (l, 0, j)),
        out_shape=jax.ShapeDtypeStruct((depth, r, n3), F32),
        compiler_params=_cparams("parallel", "parallel"),
        name="modulation",
    )(c_all, ada_w, ada_b.reshape(depth, 1, n3))


def _rope_tables(L, head_dim, reps):
    rows_n = L // GRID_W
    row = jnp.repeat(jnp.arange(rows_n), GRID_W).astype(F32)
    col = jnp.tile(jnp.arange(GRID_W), rows_n).astype(F32)
    axis_dim = head_dim // 2
    inv = ROPE_THETA ** (-jnp.arange(0, axis_dim, 2, dtype=F32) / axis_dim)
    ar, ac = row[:, None] * inv[None], col[:, None] * inv[None]
    z = jnp.zeros_like(ar)
    cos = jnp.concatenate([jnp.cos(ar), jnp.cos(ar), jnp.cos(ac), jnp.cos(ac)], axis=-1)
    s_lo = jnp.concatenate([-jnp.sin(ar), z, -jnp.sin(ac), z], axis=-1)
    s_hi = jnp.concatenate([z, jnp.sin(ar), z, jnp.sin(ac)], axis=-1)
    return tuple(jnp.tile(t, (1, reps)) for t in (cos, s_lo, s_hi))


def _rope_block(x, cos, s_lo, s_hi, half):
    fwd = pltpu.roll(x, LANE - half, axis=1)
    bwd = pltpu.roll(x, half, axis=1)
    return x * cos + fwd * s_lo + bwd * s_hi


def _inproj_kernel(*refs, rope):
    x_ref, sc_ref, sh_ref, ng_ref, w_ref, qg_g_ref, kg_g_ref = refs[:7]
    if rope:
        tabs = refs[7:13]
        outs = refs[13:]
    else:
        tabs = None
        outs = refs[7:]
    qa_o, ka_o, va_o, ga_o, qg_o, kg_o, vg_o, gg_o, uh_o, gh_o = outs

    x = x_ref[...]
    h = _rms_rows(x, ng_ref[...])
    h = h * (1.0 + sc_ref[...]) + sh_ref[...]
    hb = h.astype(BF16)

    def proj(group):
        off = COL_OFFS[group]
        return jnp.dot(hb, w_ref[:, off:off + COL_SIZES[group]], preferred_element_type=F32)

    def blocks(v):
        return [(slice(b * LANE, (b + 1) * LANE), v[:, b * LANE:(b + 1) * LANE])
                for b in range(v.shape[1] // LANE)]

    def rope_da(v):
        if not rope:
            return v
        return _rope_block(v, tabs[0][...], tabs[1][...], tabs[2][...], DA_QK_DIM // 4)

    def rope_gq(v):
        if not rope:
            return v
        return _rope_block(v, tabs[3][...], tabs[4][...], tabs[5][...], GQ_DIM // 4)

    for sl, v in blocks(proj(0)):
        qa_o[:, sl] = (rope_da(v) * (DA_QK_DIM ** -0.5)).astype(qa_o.dtype)
    for sl, v in blocks(proj(1)):
        ka_o[:, sl] = rope_da(v).astype(ka_o.dtype)
    va_o[...] = proj(2).astype(va_o.dtype)
    ga_o[...] = _silu(proj(3)).astype(ga_o.dtype)
    for sl, v in blocks(proj(4)):
        qg_o[:, sl] = (rope_gq(_rms_rows(v, qg_g_ref[...])) * (GQ_DIM ** -0.5)).astype(qg_o.dtype)
    for sl, v in blocks(proj(5)):
        kg_o[:, sl] = rope_gq(_rms_rows(v, kg_g_ref[...])).astype(kg_o.dtype)
    vg_o[...] = proj(6).astype(vg_o.dtype)
    gg_o[...] = _silu(proj(7)).astype(gg_o.dtype)
    uh_o[...] = proj(8)
    gh_o[...] = _silu(proj(9)).astype(gh_o.dtype)


def _inproj(xt, scale, shift, norm_g, w_bf, q_g, k_g, tabs):
    B, T, D = xt.shape
    tm = min(512, T)
    rope = tabs is not None
    row = lambda b, i: (b, i, 0)
    const2 = lambda b, i: (0, 0)
    in_specs = [pl.BlockSpec((None, tm, D), row),
                pl.BlockSpec((None, 1, D), lambda b, i: (b, 0, 0)),
                pl.BlockSpec((None, 1, D), lambda b, i: (b, 0, 0)),
                pl.BlockSpec((1, D), const2),
                pl.BlockSpec((D, D_IN), const2, pipeline_mode=pl.Buffered(1)),
                pl.BlockSpec((1, GQ_DIM), const2),
                pl.BlockSpec((1, GQ_DIM), const2)]
    args = [xt, scale, shift, norm_g.reshape(1, D), w_bf, q_g.reshape(1, GQ_DIM), k_g.reshape(1, GQ_DIM)]
    if rope:
        in_specs += [pl.BlockSpec((tm, LANE), lambda b, i: (i, 0))] * 6
        args += list(tabs)
    widths = (DA_WIDTH, DA_WIDTH, DA_WIDTH, DA_WIDTH, GQ_WIDTH, GQ_KV_HEADS * GQ_DIM,
              GQ_KV_HEADS * GQ_DIM, GQ_WIDTH, HY_U, HY_WIDTH)
    dtypes = (BF16, BF16, BF16, BF16, BF16, BF16, BF16, BF16, F32, BF16)
    return pl.pallas_call(
        functools.partial(_inproj_kernel, rope=rope),
        grid=(B, T // tm),
        in_specs=in_specs,
        out_specs=[pl.BlockSpec((None, tm, w), row) for w in widths],
        out_shape=[jax.ShapeDtypeStruct((B, T, w), dt) for w, dt in zip(widths, dtypes)],
        compiler_params=_cparams("parallel", "parallel"),
        name="inproj_rope" if rope else "inproj_ctx",
    )(*args)


def _softmax_pv(qs, k, v):
    s = lax.dot_general(qs, k, (((1,), (1,)), ((), ())), preferred_element_type=F32)
    m = jnp.max(s, axis=-1, keepdims=True)
    p = jnp.exp(s - m)
    l = jnp.sum(p, axis=-1, keepdims=True)
    pv = jnp.dot(p.astype(BF16), v, preferred_element_type=F32)
    return pv / l


def _da_kernel(lam_ref, q_ref, k_ref, v_ref, gate_ref, sg_ref, o_ref, *, lam_init):
    tq = q_ref.shape[0]
    lm = lam_ref[...]
    lam = (jnp.exp(jnp.sum(lm[0:1] * lm[1:2], axis=-1, keepdims=True))
           - jnp.exp(jnp.sum(lm[2:3] * lm[3:4], axis=-1, keepdims=True)) + lam_init)
    ch = min(DA_CHAIN_ROWS, tq)
    for r in range(tq // ch):
        rows = slice(r * ch, (r + 1) * ch)
        q = q_ref[rows, :]
        lane = lax.broadcasted_iota(jnp.int32, q.shape, 1)
        zero = jnp.zeros_like(q)
        o1 = _softmax_pv(jnp.where(lane < DA_QK_DIM, q, zero), k_ref[...], v_ref[...])
        o2 = _softmax_pv(jnp.where(lane >= DA_QK_DIM, q, zero), k_ref[...], v_ref[...])
        o = _rms_rows(o1 - lam * o2, sg_ref[...]) * (1.0 - lam_init)
        o_ref[rows, :] = (o * gate_ref[rows, :].astype(F32)).astype(o_ref.dtype)


def _da_attention(lam_p, q, k, v, gate, subln_g, lam_init):
    B, Tq, _ = q.shape
    Tk = k.shape[1]
    tq = min(2 * DA_CHAIN_ROWS, Tq)
    qmap = lambda b, h, i: (b, i, h)
    kmap = lambda b, h, i: (b, 0, h)
    return pl.pallas_call(
        functools.partial(_da_kernel, lam_init=lam_init),
        grid=(B, DA_HEADS, Tq // tq),
        in_specs=[pl.BlockSpec((4, DA_QK_DIM), lambda b, h, i: (0, 0)),
                  pl.BlockSpec((None, tq, LANE), qmap),
                  pl.BlockSpec((None, Tk, LANE), kmap),
                  pl.BlockSpec((None, Tk, LANE), kmap),
                  pl.BlockSpec((None, tq, LANE), qmap),
                  pl.BlockSpec((1, LANE), lambda b, h, i: (0, 0))],
        out_specs=pl.BlockSpec((None, tq, LANE), qmap),
        out_shape=jax.ShapeDtypeStruct((B, Tq, DA_WIDTH), BF16),
        compiler_params=_cparams("parallel", "parallel", "arbitrary"),
        name="diff_attention",
    )(lam_p, q, k, v, gate, subln_g.reshape(1, DA_V_DIM))


def _gq_kernel(q_ref, k_ref, v_ref, o_ref):
    for r in range(GQ_GROUP):
        sl = slice(r * LANE, (r + 1) * LANE)
        o_ref[:, sl] = _softmax_pv(q_ref[:, sl], k_ref[...], v_ref[...]).astype(o_ref.dtype)


def _gq_attention(q, k, v):
    B, Tq, _ = q.shape
    Tk = k.shape[1]
    tq = min(256, Tq)
    gw = GQ_GROUP * GQ_DIM
    qmap = lambda b, g, i: (b, i, g)
    kmap = lambda b, g, i: (b, 0, g)
    return pl.pallas_call(
        _gq_kernel,
        grid=(B, GQ_KV_HEADS, Tq // tq),
        in_specs=[pl.BlockSpec((None, tq, gw), qmap),
                  pl.BlockSpec((None, Tk, LANE), kmap),
                  pl.BlockSpec((None, Tk, LANE), kmap)],
        out_specs=pl.BlockSpec((None, tq, gw), qmap),
        out_shape=jax.ShapeDtypeStruct((B, Tq, GQ_WIDTH), F32),
        compiler_params=_cparams("parallel", "parallel", "arbitrary"),
        name="gqa_attention",
    )(q, k, v)


def _hy_pre_kernel(u0_ref, u1_ref, u2_ref, w0_ref, w1_ref, w2_ref, b0_ref, b1_ref, b2_ref,
                   x0_ref, z_ref, zb_ref):
    T = u0_ref.shape[0]
    row = lax.broadcasted_iota(jnp.int32, u0_ref.shape, 0)

    def sconv(u_ref, w_ref, b_ref):
        u = u_ref[...]
        w = w_ref[...]
        prev = jnp.where(row == 0, 0.0, pltpu.roll(u, 1, axis=0))
        nxt = jnp.where(row == T - 1, 0.0, pltpu.roll(u, T - 1, axis=0))
        return prev * w[0:1] + u * w[1:2] + nxt * w[2:3] + b_ref[...]

    x0_ref[...] = sconv(u0_ref, w0_ref, b0_ref)
    z = sconv(u2_ref, w2_ref, b2_ref) * sconv(u1_ref, w1_ref, b1_ref)
    z_ref[...] = z
    zb_ref[...] = z.astype(BF16)


def _hy_pre(uh, short_w, short_b):
    B, T, _ = uh.shape
    nb = HY_WIDTH // LANE
    u_specs = [pl.BlockSpec((None, T, LANE), lambda b, j, p=p: (b, 0, p * nb + j)) for p in range(3)]
    w_specs = [pl.BlockSpec((3, LANE), lambda b, j, p=p: (0, p * nb + j)) for p in range(3)]
    b_specs = [pl.BlockSpec((1, LANE), lambda b, j, p=p: (0, p * nb + j)) for p in range(3)]
    out_spec = pl.BlockSpec((None, T, LANE), lambda b, j: (b, 0, j))
    sb = short_b.reshape(1, HY_U)
    return pl.pallas_call(
        _hy_pre_kernel,
        grid=(B, nb),
        in_specs=u_specs + w_specs + b_specs,
        out_specs=[out_spec] * 3,
        out_shape=[jax.ShapeDtypeStruct((B, T, HY_WIDTH), F32),
                   jax.ShapeDtypeStruct((B, T, HY_WIDTH), F32),
                   jax.ShapeDtypeStruct((B, T, HY_WIDTH), BF16)],
        compiler_params=_cparams("parallel", "parallel"),
        name="hyena_short_conv",
    )(uh, uh, uh, short_w, short_w, short_w, sb, sb, sb)


def _filter_kernel(z_ref, w1_ref, b1_ref, w2_ref, b2_ref, w3_ref, b3_ref, w4f_ref, w4b_ref,
                   fr_ref, t_ref, dl_ref, fe_ref, fo_ref):
    hp = lax.Precision.HIGHEST
    fr = fr_ref[...]
    h = jnp.sin(fr * (jnp.dot(z_ref[...], w1_ref[...], preferred_element_type=F32, precision=hp) + b1_ref[...]))
    h = jnp.sin(fr * (jnp.dot(h, w2_ref[...], preferred_element_type=F32, precision=hp) + b2_ref[...]))
    h = jnp.sin(fr * (jnp.dot(h, w3_ref[...], preferred_element_type=F32, precision=hp) + b3_ref[...]))
    decay = jnp.exp(-t_ref[...] * jnp.abs(dl_ref[...]))
    fwd = jnp.dot(h, w4f_ref[...], preferred_element_type=F32, precision=hp) * decay
    bwd = jnp.dot(h, w4b_ref[...], preferred_element_type=F32, precision=hp) * decay
    row = lax.broadcasted_iota(jnp.int32, bwd.shape, 0)
    bwd = jnp.where(row == 0, 0.0, bwd)
    ss = jnp.sum(fwd * fwd + bwd * bwd, axis=0, keepdims=True)
    sc = lax.rsqrt(ss + EPS)
    fe_ref[...] = ((fwd + bwd) * sc).astype(fe_ref.dtype)
    fo_ref[...] = ((fwd - bwd) * sc).astype(fo_ref.dtype)


def _hyena_filter_eo(T, w1, b1, w2, b2, w3, b3, w4, freq):
    t = jnp.linspace(0.0, 1.0, T, dtype=F32)[:, None]
    w = 2.0 * math.pi * jnp.arange(T, dtype=F32)[:, None] / T
    f = jnp.linspace(1e-4, HY_BANDS - 1, HY_BANDS, dtype=F32)[None]
    z = jnp.concatenate([t, jnp.cos(f * w), -jnp.sin(f * w)], axis=-1)
    z = jnp.pad(z, ((0, 0), (0, LANE - HY_EMB)))
    w1p = jnp.pad(w1, ((0, LANE - HY_EMB), (0, 0)))
    deltas = jnp.linspace(HY_MIN_DECAY, HY_MAX_DECAY, HY_WIDTH, dtype=F32)[None]
    nb = HY_WIDTH // LANE
    c2 = lambda j: (0, 0)
    vec = lambda a: a.reshape(1, HY_FFN)
    return pl.pallas_call(
        _filter_kernel,
        grid=(nb,),
        in_specs=[pl.BlockSpec((T, LANE), c2),
                  pl.BlockSpec((LANE, HY_FFN), c2), pl.BlockSpec((1, HY_FFN), c2),
                  pl.BlockSpec((HY_FFN, HY_FFN), c2), pl.BlockSpec((1, HY_FFN), c2),
                  pl.BlockSpec((HY_FFN, HY_FFN), c2), pl.BlockSpec((1, HY_FFN), c2),
                  pl.BlockSpec((HY_FFN, LANE), lambda j: (0, j)),
                  pl.BlockSpec((HY_FFN, LANE), lambda j: (0, nb + j)),
                  pl.BlockSpec((1, HY_FFN), c2),
                  pl.BlockSpec((T, 1), c2),
                  pl.BlockSpec((1, LANE), lambda j: (0, j))],
        out_specs=[pl.BlockSpec((T, LANE), lambda j: (0, j))] * 2,
        out_shape=[jax.ShapeDtypeStruct((T, HY_WIDTH), BF16)] * 2,
        compiler_params=_cparams("parallel"),
        name="hyena_filter",
    )(z, w1p, vec(b1), w2, vec(b2), w3, vec(b3), w4, w4, vec(freq), t, deltas)


def _dft_tile_rows(T):
    return min(1024, 2 * T)


def _dft_tables(T):
    n = 2 * T
    tr = _dft_tile_rows(T)
    half = tr // 2
    r = np.arange(n)
    local = r % tr
    k = (r // tr) * half + local % half
    is_sin = local >= half
    nyq = is_sin & (k == 0)
    k = np.where(nyq, T, k)
    is_sin = is_sin & ~nyq
    lo = 64 if T % 64 == 0 else 1
    hi = T // lo
    ang = 2.0 * np.pi / n
    a_lo = ((k[:, None] * np.arange(lo)[None]) % n) * ang
    a_hi = ((k[:, None] * (np.arange(hi)[None] * lo)) % n) * ang
    sel = is_sin[:, None]
    x_lo = jnp.asarray(np.where(sel, np.sin(a_lo), np.cos(a_lo)), F32)
    w_lo = jnp.asarray(np.where(sel, np.cos(a_lo), -np.sin(a_lo)), F32)
    c_hi = jnp.asarray(np.cos(a_hi), F32)
    s_hi = jnp.asarray(np.sin(a_hi), F32)
    a = (x_lo[:, None, :] * c_hi[:, :, None] + w_lo[:, None, :] * s_hi[:, :, None]).reshape(n, T)
    at = (x_lo.T[None, :, :] * c_hi.T[:, None, :] + w_lo.T[None, :, :] * s_hi.T[:, None, :]).reshape(T, n)
    return a.astype(BF16), at.astype(BF16)


def _dft_raw_kernel(a_ref, z_ref, o_ref):
    o_ref[...] = jnp.dot(a_ref[...], z_ref[...], preferred_element_type=F32)


def _dft_raw(a, f):
    n, T = a.shape
    N = f.shape[1]
    tr = _dft_tile_rows(T)
    return pl.pallas_call(
        _dft_raw_kernel,
        grid=(n // tr,),
        in_specs=[pl.BlockSpec((tr, T), lambda i: (i, 0)),
                  pl.BlockSpec((T, N), lambda i: (0, 0))],
        out_specs=pl.BlockSpec((tr, N), lambda i: (i, 0)),
        out_shape=jax.ShapeDtypeStruct((n, N), F32),
        compiler_params=_cparams("parallel"),
        name="dft_filter",
    )(a, f)


def _filter_spectrum(a, fe, fo):
    n, T = a.shape
    tr = _dft_tile_rows(T)
    o = _dft_raw(a, jnp.concatenate([fe, fo], axis=1))
    o = o.reshape(n // tr, 2, tr // 2, 2 * HY_WIDTH)
    cosp = o[:, 0].reshape(T, 2 * HY_WIDTH)
    sinp = o[:, 1].reshape(T, 2 * HY_WIDTH)
    hr = cosp[:, :HY_WIDTH]
    hi = -sinp[:, HY_WIDTH:]
    hnyq = sinp[0:1, :HY_WIDTH]
    first = (jnp.arange(T) == 0)[:, None]
    ck = jnp.where(first, 1.0 / n, 2.0 / n)
    t1 = ck * hr
    t2 = jnp.where(first, 0.0, ck * hi)
    t4 = jnp.where(first, hnyq / n, ck * hr)
    return t1, t2, t4


def _dft_fwd_kernel(a_ref, z_ref, t1_ref, t2_ref, t4_ref, y_ref):
    half = a_ref.shape[0] // 2
    u = jnp.dot(a_ref[...], z_ref[...], preferred_element_type=F32)
    re, im = u[:half], u[half:]
    t2 = t2_ref[...]
    y_ref[:half, :] = (re * t1_ref[...] + im * t2).astype(y_ref.dtype)
    y_ref[half:, :] = (im * t4_ref[...] - re * t2).astype(y_ref.dtype)


def _dft_fwd(a, zb, t1, t2, t4):
    n, T = a.shape
    B = zb.shape[0]
    tr = _dft_tile_rows(T)
    tmap = lambda i, b: (i, 0)
    return pl.pallas_call(
        _dft_fwd_kernel,
        grid=(n // tr, B),
        in_specs=[pl.BlockSpec((tr, T), tmap),
                  pl.BlockSpec((None, T, HY_WIDTH), lambda i, b: (b, 0, 0)),
                  pl.BlockSpec((tr // 2, HY_WIDTH), tmap),
                  pl.BlockSpec((tr // 2, HY_WIDTH), tmap),
                  pl.BlockSpec((tr // 2, HY_WIDTH), tmap)],
        out_specs=pl.BlockSpec((None, tr, HY_WIDTH), lambda i, b: (b, i, 0)),
        out_shape=jax.ShapeDtypeStruct((B, n, HY_WIDTH), BF16),
        compiler_params=_cparams("parallel", "arbitrary"),
        name="dft_forward",
    )(a, zb, t1, t2, t4)


def _dft_inv_kernel(at_ref, y_ref, x0_ref, z_ref, skip_ref, o_ref):
    y = jnp.dot(at_ref[...], y_ref[...], preferred_element_type=F32)
    o_ref[...] = x0_ref[...] * (y + z_ref[...] * skip_ref[...])


def _dft_inv(at, y, x0, z, skip):
    T, n = at.shape
    B = y.shape[0]
    tt = min(512, T)
    rmap = lambda i, b: (b, i, 0)
    return pl.pallas_call(
        _dft_inv_kernel,
        grid=(T // tt, B),
        in_specs=[pl.BlockSpec((tt, n), lambda i, b: (i, 0)),
                  pl.BlockSpec((None, n, HY_WIDTH), lambda i, b: (b, 0, 0)),
                  pl.BlockSpec((None, tt, HY_WIDTH), rmap),
                  pl.BlockSpec((None, tt, HY_WIDTH), rmap),
                  pl.BlockSpec((1, HY_WIDTH), lambda i, b: (0, 0))],
        out_specs=pl.BlockSpec((None, tt, HY_WIDTH), rmap),
        out_shape=jax.ShapeDtypeStruct((B, T, HY_WIDTH), F32),
        compiler_params=_cparams("parallel", "arbitrary"),
        name="dft_inverse",
    )(at, y, x0, z, skip.reshape(1, HY_WIDTH))


def _hyena(uh, dft, spectrum, short_w, short_b, skip):
    a, at = dft
    x0, z, zb = _hy_pre(uh, short_w, short_b)
    y = _dft_fwd(a, zb, *spectrum)
    return _dft_inv(at, y, x0, z, skip)


def _outproj_kernel(ya_ref, ogq_ref, gg_ref, ohy_ref, gh_ref, x_ref, gate_ref, w_ref,
                    gqg_ref, hyg_ref, fg_ref, o_ref, *, final):
    yg = _rms_rows(ogq_ref[...], gqg_ref[...]) * gg_ref[...].astype(F32)
    yh = _rms_rows(ohy_ref[...], hyg_ref[...]) * gh_ref[...].astype(F32)
    a0, a1, a2 = DA_WIDTH, DA_WIDTH + GQ_WIDTH, D_MIX
    y = jnp.dot(ya_ref[...], w_ref[0:a0, :], preferred_element_type=F32)
    y = y + jnp.dot(yg.astype(BF16), w_ref[a0:a1, :], preferred_element_type=F32)
    y = y + jnp.dot(yh.astype(BF16), w_ref[a1:a2, :], preferred_element_type=F32)
    out = x_ref[...] + gate_ref[...] * y
    if final:
        out = _rms_rows(out, fg_ref[...])
    o_ref[...] = out


def _outproj(ya, ogq, gg, ohy, gh, xt, gate, w_bf, gq_out_g, hy_out_g, final_g, final):
    B, T, D = xt.shape
    tm = min(512, T)
    row = lambda b, i: (b, i, 0)
    c2 = lambda b, i: (0, 0)
    return pl.pallas_call(
        functools.partial(_outproj_kernel, final=final),
        grid=(B, T // tm),
        in_specs=[pl.BlockSpec((None, tm, DA_WIDTH), row),
                  pl.BlockSpec((None, tm, GQ_WIDTH), row),
                  pl.BlockSpec((None, tm, GQ_WIDTH), row),
                  pl.BlockSpec((None, tm, HY_WIDTH), row),
                  pl.BlockSpec((None, tm, HY_WIDTH), row),
                  pl.BlockSpec((None, tm, D), row),
                  pl.BlockSpec((None, 1, D), lambda b, i: (b, 0, 0)),
                  pl.BlockSpec((D_MIX, D), c2),
                  pl.BlockSpec((1, GQ_WIDTH), c2),
                  pl.BlockSpec((1, HY_WIDTH), c2),
                  pl.BlockSpec((1, D), c2)],
        out_specs=pl.BlockSpec((None, tm, D), row),
        out_shape=jax.ShapeDtypeStruct((B, T, D), F32),
        compiler_params=_cparams("parallel", "parallel"),
        name="outproj_final" if final else "outproj",
    )(ya, ogq, gg, ohy, gh, xt, gate, w_bf, gq_out_g.reshape(1, GQ_WIDTH),
      hy_out_g.reshape(1, HY_WIDTH), final_g.reshape(1, D))


def kernel(x, c, ctx, c_ctx, ada_w, ada_b, norm_g, w_in, w_out, da_lambda, da_subln_g, gq_q_g, gq_k_g, gq_out_g, hy_short_w, hy_short_b, hy_w1, hy_b1, hy_w2, hy_b2, hy_w3, hy_b3, hy_w4, hy_freq, hy_bias, hy_out_g, final_g):
    B, L, D = x.shape
    Lc = ctx.shape[1]
    depth = ada_w.shape[0]
    assert L % GRID_W == 0 and D_IN == w_in.shape[2]

    tabs = _rope_tables(L, DA_QK_DIM, 2) + _rope_tables(L, GQ_DIM, 1)
    dft_lat = _dft_tables(L)
    dft_ctx = _dft_tables(Lc)

    n_mod = B + 1
    pad = (-n_mod) % 8
    c_all = jnp.concatenate([c, c_ctx[None], jnp.zeros((pad, D), F32)], axis=0)
    mod = _modulation(c_all, ada_w, ada_b)

    w_in_bf = w_in.astype(BF16)
    w_out_bf = w_out.astype(BF16)

    xc = ctx
    for l in range(depth):
        update_ctx = l < depth - 1
        lam_init = 0.8 - 0.6 * math.exp(-0.3 * l)
        shift, scale, gate = (mod[l, :B, i * D:(i + 1) * D][:, None, :] for i in range(3))
        shift_c, scale_c, gate_c = (jnp.broadcast_to(mod[l, B:B + 1, i * D:(i + 1) * D][:, None, :], (B, 1, D))
                                    for i in range(3))

        qa, ka, va, ga, qg, kg, vg, gg, uh, gh = _inproj(
            x, scale, shift, norm_g[l], w_in_bf[l], gq_q_g[l], gq_k_g[l], tabs)
        qa_c, ka_c, va_c, ga_c, qg_c, kg_c, vg_c, gg_c, uh_c, gh_c = _inproj(
            xc, scale_c, shift_c, norm_g[l], w_in_bf[l], gq_q_g[l], gq_k_g[l], None)

        cat = lambda a_c, a: jnp.concatenate([a_c, a], axis=1)
        ya = _da_attention(da_lambda[l], qa, cat(ka_c, ka), cat(va_c, va), ga, da_subln_g[l], lam_init)
        ogq = _gq_attention(qg, cat(kg_c, kg), cat(vg_c, vg))

        filt = (hy_w1[l], hy_b1[l], hy_w2[l], hy_b2[l], hy_w3[l], hy_b3[l], hy_w4[l], hy_freq[l])
        spec = _filter_spectrum(dft_lat[0], *_hyena_filter_eo(L, *filt))
        ohy = _hyena(uh, dft_lat, spec, hy_short_w[l], hy_short_b[l], hy_bias[l])

        if update_ctx:
            ya_c = _da_attention(da_lambda[l], qa_c, ka_c, va_c, ga_c, da_subln_g[l], lam_init)
            ogq_c = _gq_attention(qg_c, kg_c, vg_c)
            spec_c = _filter_spectrum(dft_ctx[0], *_hyena_filter_eo(Lc, *filt))
            ohy_c = _hyena(uh_c, dft_ctx, spec_c, hy_short_w[l], hy_short_b[l], hy_bias[l])
            xc = _outproj(ya_c, ogq_c, gg_c, ohy_c, gh_c, xc, gate_c, w_out_bf[l],
                          gq_out_g[l], hy_out_g[l], final_g, False)

        x = _outproj(ya, ogq, gg, ohy, gh, x, gate, w_out_bf[l],
                     gq_out_g[l], hy_out_g[l], final_g, l == depth - 1)
    return x
```

```python
import functools
import math

import jax
import jax.numpy as jnp
import numpy as np
from jax import lax
from jax.experimental import pallas as pl
from jax.experimental.pallas import tpu as pltpu

F32 = jnp.float32
BF16 = jnp.bfloat16

GRID_W = 64
EPS = 1e-6
ROPE_THETA = 10000.0
DA_HEADS = 4
DA_QK_DIM = 64
DA_V_DIM = 2 * DA_QK_DIM
DA_WIDTH = DA_HEADS * DA_V_DIM
GQ_HEADS = 8
GQ_KV_HEADS = 2
GQ_GROUP = GQ_HEADS // GQ_KV_HEADS
GQ_DIM = 128
GQ_WIDTH = GQ_HEADS * GQ_DIM
HY_WIDTH = 512
HY_EMB = 33
HY_BANDS = (HY_EMB - 1) // 2
HY_FFN = 64
HY_MAX_DECAY = math.log(1e-2) / 0.3
HY_MIN_DECAY = math.log(1e-2) / 1.5
HY_U = 3 * HY_WIDTH
D_MIX = DA_WIDTH + GQ_WIDTH + HY_WIDTH
COL_SIZES = (DA_WIDTH, DA_WIDTH, DA_WIDTH, DA_WIDTH,
             GQ_WIDTH, GQ_KV_HEADS * GQ_DIM, GQ_KV_HEADS * GQ_DIM, GQ_WIDTH,
             HY_U, HY_WIDTH)
COL_OFFS = tuple(int(v) for v in np.cumsum((0,) + COL_SIZES))
D_IN = COL_OFFS[-1]

LANE = 128
ATTN_CHAIN_ROWS = 256
ATTN_CHAINS = 8
DFT_FACTOR = 64
VMEM_LIMIT = 56 * 1024 * 1024


def _cparams(*sem):
    return pltpu.CompilerParams(dimension_semantics=sem, vmem_limit_bytes=VMEM_LIMIT)


def _silu(g):
    return g / (1.0 + jnp.exp(-g))


def _rms_rows(x, g):
    return x * lax.rsqrt(jnp.mean(x * x, axis=-1, keepdims=True) + EPS) * g


def _mod_kernel(c_ref, w_ref, b_ref, o_ref):
    c = c_ref[...]
    o_ref[...] = jnp.dot(_silu(c), w_ref[...], preferred_element_type=F32,
                         precision=lax.Precision.HIGHEST) + b_ref[...]


def _modulation(c_all, ada_w, ada_b):
    depth, d, n3 = ada_w.shape
    r = c_all.shape[0]
    tn = 512
    return pl.pallas_call(
        _mod_kernel,
        grid=(depth, n3 // tn),
        in_specs=[pl.BlockSpec((r, d), lambda l, j: (0, 0)),
                  pl.BlockSpec((None, d, tn), lambda l, j: (l, 0, j)),
                  pl.BlockSpec((None, 1, tn), lambda l, j: (l, 0, j))],
        out_specs=pl.BlockSpec((None, r, tn), lambda l, j: (l, 0, j)),
        out_shape=jax.ShapeDtypeStruct((depth, r, n3), F32),
        compiler_params=_cparams("parallel", "parallel"),
        name="modulation",
    )(c_all, ada_w, ada_b.reshape(depth, 1, n3))


def _rope_tables(L, head_dim, reps):
    rows_n = L // GRID_W
    row = jnp.repeat(jnp.arange(rows_n), GRID_W).astype(F32)
    col = jnp.tile(jnp.arange(GRID_W), rows_n).astype(F32)
    axis_dim = head_dim // 2
    inv = ROPE_THETA ** (-jnp.arange(0, axis_dim, 2, dtype=F32) / axis_dim)
    ar, ac = row[:, None] * inv[None], col[:, None] * inv[None]
    z = jnp.zeros_like(ar)
    cos = jnp.concatenate([jnp.cos(ar), jnp.cos(ar), jnp.cos(ac), jnp.cos(ac)], axis=-1)
    s_lo = jnp.concatenate([-jnp.sin(ar), z, -jnp.sin(ac), z], axis=-1)
    s_hi = jnp.concatenate([z, jnp.sin(ar), z, jnp.sin(ac)], axis=-1)
    return tuple(jnp.tile(t, (1, reps)) for t in (cos, s_lo, s_hi))


def _rope_block(x, cos, s_lo, s_hi, half):
    fwd = pltpu.roll(x, LANE - half, axis=1)
    bwd = pltpu.roll(x, half, axis=1)
    return x * cos + fwd * s_lo + bwd * s_hi


def _inproj_kernel(*refs, rope):
    x_ref, sc_ref, sh_ref, ng_ref, w_ref, qg_g_ref, kg_g_ref = refs[:7]
    if rope:
        tabs = refs[7:13]
        outs = refs[13:]
    else:
        tabs = None
        outs = refs[7:]
    qa_o, ka_o, va_o, ga_o, qg_o, kg_o, vg_o, gg_o, uh_o, gh_o = outs

    x = x_ref[...]
    h = _rms_rows(x, ng_ref[...])
    h = h * (1.0 + sc_ref[...]) + sh_ref[...]
    hb = h.astype(BF16)

    def proj(group):
        off = COL_OFFS[group]
        return jnp.dot(hb, w_ref[:, off:off + COL_SIZES[group]], preferred_element_type=F32)

    def blocks(v):
        return [(slice(b * LANE, (b + 1) * LANE), v[:, b * LANE:(b + 1) * LANE])
                for b in range(v.shape[1] // LANE)]

    def rope_da(v):
        if not rope:
            return v
        return _rope_block(v, tabs[0][...], tabs[1][...], tabs[2][...], DA_QK_DIM // 4)

    def rope_gq(v):
        if not rope:
            return v
        return _rope_block(v, tabs[3][...], tabs[4][...], tabs[5][...], GQ_DIM // 4)

    for sl, v in blocks(proj(0)):
        qa_o[:, sl] = (rope_da(v) * (DA_QK_DIM ** -0.5)).astype(qa_o.dtype)
    for sl, v in blocks(proj(1)):
        ka_o[:, sl] = rope_da(v).astype(ka_o.dtype)
    va_o[...] = proj(2).astype(va_o.dtype)
    ga_o[...] = _silu(proj(3)).astype(ga_o.dtype)
    for sl, v in blocks(proj(4)):
        qg_o[:, sl] = (rope_gq(_rms_rows(v, qg_g_ref[...])) * (GQ_DIM ** -0.5)).astype(qg_o.dtype)
    for sl, v in blocks(proj(5)):
        kg_o[:, sl] = rope_gq(_rms_rows(v, kg_g_ref[...])).astype(kg_o.dtype)
    vg_o[...] = proj(6).astype(vg_o.dtype)
    gg_o[...] = _silu(proj(7)).astype(gg_o.dtype)
    uh_o[...] = proj(8)
    gh_o[...] = _silu(proj(9)).astype(gh_o.dtype)


def _inproj(xt, scale, shift, norm_g, w_bf, q_g, k_g, tabs):
    B, T, D = xt.shape
    tm = min(512, T)
    rope = tabs is not None
    row = lambda b, i: (b, i, 0)
    const2 = lambda b, i: (0, 0)
    in_specs = [pl.BlockSpec((None, tm, D), row),
                pl.BlockSpec((None, 1, D), lambda b, i: (b, 0, 0)),
                pl.BlockSpec((None, 1, D), lambda b, i: (b, 0, 0)),
                pl.BlockSpec((1, D), const2),
                pl.BlockSpec((D, D_IN), const2, pipeline_mode=pl.Buffered(1)),
                pl.BlockSpec((1, GQ_DIM), const2),
                pl.BlockSpec((1, GQ_DIM), const2)]
    args = [xt, scale, shift, norm_g.reshape(1, D), w_bf, q_g.reshape(1, GQ_DIM), k_g.reshape(1, GQ_DIM)]
    if rope:
        in_specs += [pl.BlockSpec((tm, LANE), lambda b, i: (i, 0))] * 6
        args += list(tabs)
    widths = (DA_WIDTH, DA_WIDTH, DA_WIDTH, DA_WIDTH, GQ_WIDTH, GQ_KV_HEADS * GQ_DIM,
              GQ_KV_HEADS * GQ_DIM, GQ_WIDTH, HY_U, HY_WIDTH)
    dtypes = (BF16, BF16, BF16, BF16, BF16, BF16, BF16, BF16, F32, BF16)
    return pl.pallas_call(
        functools.partial(_inproj_kernel, rope=rope),
        grid=(B, T // tm),
        in_specs=in_specs,
        out_specs=[pl.BlockSpec((None, tm, w), row) for w in widths],
        out_shape=[jax.ShapeDtypeStruct((B, T, w), dt) for w, dt in zip(widths, dtypes)],
        compiler_params=_cparams("parallel", "parallel"),
        name="inproj_rope" if rope else "inproj_ctx",
    )(*args)


def _softmax_pv(qs, k, v):
    s = lax.dot_general(qs, k, (((1,), (1,)), ((), ())), preferred_element_type=F32)
    m = jnp.max(s, axis=-1, keepdims=True)
    p = jnp.exp(s - m)
    l = jnp.sum(p, axis=-1, keepdims=True)
    pv = jnp.dot(p.astype(BF16), v, preferred_element_type=F32)
    return pv / l


def _da_kernel(lam_ref, q_ref, k_ref, v_ref, gate_ref, sg_ref, o_ref, *, lam_init):
    tq = q_ref.shape[0]
    lm = lam_ref[...]
    lam = (jnp.exp(jnp.sum(lm[0:1] * lm[1:2], axis=-1, keepdims=True))
           - jnp.exp(jnp.sum(lm[2:3] * lm[3:4], axis=-1, keepdims=True)) + lam_init)
    ch = min(ATTN_CHAIN_ROWS, tq)
    for r in range(tq // ch):
        rows = slice(r * ch, (r + 1) * ch)
        q = q_ref[rows, :]
        lane = lax.broadcasted_iota(jnp.int32, q.shape, 1)
        zero = jnp.zeros_like(q)
        o1 = _softmax_pv(jnp.where(lane < DA_QK_DIM, q, zero), k_ref[...], v_ref[...])
        o2 = _softmax_pv(jnp.where(lane >= DA_QK_DIM, q, zero), k_ref[...], v_ref[...])
        o = _rms_rows(o1 - lam * o2, sg_ref[...]) * (1.0 - lam_init)
        o_ref[rows, :] = (o * gate_ref[rows, :].astype(F32)).astype(o_ref.dtype)


def _da_attention(lam_p, q, k, v, gate, subln_g, lam_init):
    B, Tq, _ = q.shape
    Tk = k.shape[1]
    tq = min(ATTN_CHAINS * ATTN_CHAIN_ROWS // 2, Tq)
    qmap = lambda b, h, i: (b, i, h)
    kmap = lambda b, h, i: (b, 0, h)
    return pl.pallas_call(
        functools.partial(_da_kernel, lam_init=lam_init),
        grid=(B, DA_HEADS, Tq // tq),
        in_specs=[pl.BlockSpec((4, DA_QK_DIM), lambda b, h, i: (0, 0)),
                  pl.BlockSpec((None, tq, LANE), qmap),
                  pl.BlockSpec((None, Tk, LANE), kmap),
                  pl.BlockSpec((None, Tk, LANE), kmap),
                  pl.BlockSpec((None, tq, LANE), qmap),
                  pl.BlockSpec((1, LANE), lambda b, h, i: (0, 0))],
        out_specs=pl.BlockSpec((None, tq, LANE), qmap),
        out_shape=jax.ShapeDtypeStruct((B, Tq, DA_WIDTH), BF16),
        compiler_params=_cparams("parallel", "parallel", "arbitrary"),
        name="diff_attention",
    )(lam_p, q, k, v, gate, subln_g.reshape(1, DA_V_DIM))


def _gq_kernel(q_ref, k_ref, v_ref, o_ref):
    tq = q_ref.shape[0]
    ch = min(ATTN_CHAIN_ROWS, tq)
    for c in range(tq // ch):
        rows = slice(c * ch, (c + 1) * ch)
        for r in range(GQ_GROUP):
            sl = slice(r * LANE, (r + 1) * LANE)
            o_ref[rows, sl] = _softmax_pv(q_ref[rows, sl], k_ref[...], v_ref[...]).astype(o_ref.dtype)


def _gq_attention(q, k, v):
    B, Tq, _ = q.shape
    Tk = k.shape[1]
    tq = min(ATTN_CHAINS * ATTN_CHAIN_ROWS // GQ_GROUP, Tq)
    gw = GQ_GROUP * GQ_DIM
    qmap = lambda b, g, i: (b, i, g)
    kmap = lambda b, g, i: (b, 0, g)
    return pl.pallas_call(
        _gq_kernel,
        grid=(B, GQ_KV_HEADS, Tq // tq),
        in_specs=[pl.BlockSpec((None, tq, gw), qmap),
                  pl.BlockSpec((None, Tk, LANE), kmap),
                  pl.BlockSpec((None, Tk, LANE), kmap)],
        out_specs=pl.BlockSpec((None, tq, gw), qmap),
        out_shape=jax.ShapeDtypeStruct((B, Tq, GQ_WIDTH), F32),
        compiler_params=_cparams("parallel", "parallel", "arbitrary"),
        name="gqa_attention",
    )(q, k, v)


def _hy_pre_kernel(u0_ref, u1_ref, u2_ref, w0_ref, w1_ref, w2_ref, b0_ref, b1_ref, b2_ref,
                   x0_ref, z_ref, zb_ref):
    T = u0_ref.shape[0]
    row = lax.broadcasted_iota(jnp.int32, u0_ref.shape, 0)

    def sconv(u_ref, w_ref, b_ref):
        u = u_ref[...]
        w = w_ref[...]
        prev = jnp.where(row == 0, 0.0, pltpu.roll(u, 1, axis=0))
        nxt = jnp.where(row == T - 1, 0.0, pltpu.roll(u, T - 1, axis=0))
        return prev * w[0:1] + u * w[1:2] + nxt * w[2:3] + b_ref[...]

    x0_ref[...] = sconv(u0_ref, w0_ref, b0_ref)
    z = sconv(u2_ref, w2_ref, b2_ref) * sconv(u1_ref, w1_ref, b1_ref)
    z_ref[...] = z
    zb_ref[...] = z.astype(BF16)


def _hy_pre(uh, short_w, short_b):
    B, T, _ = uh.shape
    nb = HY_WIDTH // LANE
    u_specs = [pl.BlockSpec((None, T, LANE), lambda b, j, p=p: (b, 0, p * nb + j)) for p in range(3)]
    w_specs = [pl.BlockSpec((3, LANE), lambda b, j, p=p: (0, p * nb + j)) for p in range(3)]
    b_specs = [pl.BlockSpec((1, LANE), lambda b, j, p=p: (0, p * nb + j)) for p in range(3)]
    out_spec = pl.BlockSpec((None, T, LANE), lambda b, j: (b, 0, j))
    sb = short_b.reshape(1, HY_U)
    return pl.pallas_call(
        _hy_pre_kernel,
        grid=(B, nb),
        in_specs=u_specs + w_specs + b_specs,
        out_specs=[out_spec] * 3,
        out_shape=[jax.ShapeDtypeStruct((B, T, HY_WIDTH), F32),
                   jax.ShapeDtypeStruct((B, T, HY_WIDTH), F32),
                   jax.ShapeDtypeStruct((B, T, HY_WIDTH), BF16)],
        compiler_params=_cparams("parallel", "parallel"),
        name="hyena_short_conv",
    )(uh, uh, uh, short_w, short_w, short_w, sb, sb, sb)


def _filter_kernel(z_ref, w1_ref, b1_ref, w2_ref, b2_ref, w3_ref, b3_ref, w4f_ref, w4b_ref,
                   fr_ref, t_ref, dl_ref, fe_ref, fo_ref, h_ref):
    hp = lax.Precision.HIGHEST

    @pl.when(pl.program_id(0) == 0)
    def _():
        fr = fr_ref[...]
        h = jnp.sin(fr * (jnp.dot(z_ref[...], w1_ref[...], preferred_element_type=F32, precision=hp) + b1_ref[...]))
        h = jnp.sin(fr * (jnp.dot(h, w2_ref[...], preferred_element_type=F32, precision=hp) + b2_ref[...]))
        h_ref[...] = jnp.sin(fr * (jnp.dot(h, w3_ref[...], preferred_element_type=F32, precision=hp) + b3_ref[...]))

    h = h_ref[...]
    decay = jnp.exp(-t_ref[...] * jnp.abs(dl_ref[...]))
    fwd = jnp.dot(h, w4f_ref[...], preferred_element_type=F32, precision=hp) * decay
    bwd = jnp.dot(h, w4b_ref[...], preferred_element_type=F32, precision=hp) * decay
    row = lax.broadcasted_iota(jnp.int32, bwd.shape, 0)
    bwd = jnp.where(row == 0, 0.0, bwd)
    ss = jnp.sum(fwd * fwd + bwd * bwd, axis=0, keepdims=True)
    sc = lax.rsqrt(ss + EPS)
    fe_ref[...] = ((fwd + bwd) * sc).astype(fe_ref.dtype)
    fo_ref[...] = ((fwd - bwd) * sc).astype(fo_ref.dtype)


def _hyena_filter_eo(T, w1, b1, w2, b2, w3, b3, w4, freq):
    t = jnp.linspace(0.0, 1.0, T, dtype=F32)[:, None]
    w = 2.0 * math.pi * jnp.arange(T, dtype=F32)[:, None] / T
    f = jnp.linspace(1e-4, HY_BANDS - 1, HY_BANDS, dtype=F32)[None]
    z = jnp.concatenate([t, jnp.cos(f * w), -jnp.sin(f * w)], axis=-1)
    z = jnp.pad(z, ((0, 0), (0, LANE - HY_EMB)))
    w1p = jnp.pad(w1, ((0, LANE - HY_EMB), (0, 0)))
    deltas = jnp.linspace(HY_MIN_DECAY, HY_MAX_DECAY, HY_WIDTH, dtype=F32)[None]
    nb = HY_WIDTH // LANE
    c2 = lambda j: (0, 0)
    vec = lambda a: a.reshape(1, HY_FFN)
    return pl.pallas_call(
        _filter_kernel,
        grid=(nb,),
        in_specs=[pl.BlockSpec((T, LANE), c2),
                  pl.BlockSpec((LANE, HY_FFN), c2), pl.BlockSpec((1, HY_FFN), c2),
                  pl.BlockSpec((HY_FFN, HY_FFN), c2), pl.BlockSpec((1, HY_FFN), c2),
                  pl.BlockSpec((HY_FFN, HY_FFN), c2), pl.BlockSpec((1, HY_FFN), c2),
                  pl.BlockSpec((HY_FFN, LANE), lambda j: (0, j)),
                  pl.BlockSpec((HY_FFN, LANE), lambda j: (0, nb + j)),
                  pl.BlockSpec((1, HY_FFN), c2),
                  pl.BlockSpec((T, 1), c2),
                  pl.BlockSpec((1, LANE), lambda j: (0, j))],
        out_specs=[pl.BlockSpec((T, LANE), lambda j: (0, j))] * 2,
        out_shape=[jax.ShapeDtypeStruct((T, HY_WIDTH), BF16)] * 2,
        scratch_shapes=[pltpu.VMEM((T, HY_FFN), F32)],
        compiler_params=_cparams("arbitrary"),
        name="hyena_filter",
    )(z, w1p, vec(b1), w2, vec(b2), w3, vec(b3), w4, w4, vec(freq), t, deltas)


def _dft_tile_rows(T):
    return min(1024, 2 * T)


def _dft_tables(T):
    n = 2 * T
    tr = _dft_tile_rows(T)
    half = tr // 2
    f = DFT_FACTOR
    assert T % f == 0 and half % f == 0
    ang = 2.0 * math.pi / n

    def cs(k, s):
        a = ((k * s) % n).astype(F32) * ang
        return jnp.cos(a), jnp.sin(a)

    s = jnp.arange(T, dtype=jnp.int32)[None]
    cl, sl = cs(jnp.arange(f, dtype=jnp.int32)[:, None], s)
    ch, sh = cs(jnp.arange(T // f, dtype=jnp.int32)[:, None] * f, s)
    x = jnp.stack([cl, sl])[None, :, None]
    w = jnp.stack([-sl, cl])[None, :, None]
    ch5 = ch.reshape(n // tr, 1, half // f, 1, T)
    sh5 = sh.reshape(n // tr, 1, half // f, 1, T)
    a = (ch5 * x + sh5 * w).reshape(n, T).astype(BF16)
    nyq = jnp.where(jnp.arange(T) % 2 == 0, 1.0, -1.0).astype(BF16)
    a = a.at[half].set(nyq)

    r = np.arange(n)
    local = r % tr
    k = (r // tr) * half + local % half
    is_sin = local >= half
    is_nyq = is_sin & (k == 0)
    k = jnp.asarray(np.where(is_nyq, T, k), jnp.int32)[None]
    sel = jnp.asarray(is_sin & ~is_nyq)[None]
    c_lo, s_lo = cs(k, jnp.arange(f, dtype=jnp.int32)[:, None])
    c_hi, s_hi = cs(k, jnp.arange(T // f, dtype=jnp.int32)[:, None] * f)
    x_lo = jnp.where(sel, s_lo, c_lo)
    w_lo = jnp.where(sel, c_lo, -s_lo)
    at = (x_lo[None] * c_hi[:, None] + w_lo[None] * s_hi[:, None]).reshape(T, n).astype(BF16)
    return a, at


def _dft_raw_kernel(a_ref, z_ref, o_ref):
    o_ref[...] = jnp.dot(a_ref[...], z_ref[...], preferred_element_type=F32)


def _dft_raw(a, f):
    n, T = a.shape
    N = f.shape[1]
    tr = _dft_tile_rows(T)
    return pl.pallas_call(
        _dft_raw_kernel,
        grid=(n // tr,),
        in_specs=[pl.BlockSpec((tr, T), lambda i: (i, 0)),
                  pl.BlockSpec((T, N), lambda i: (0, 0))],
        out_specs=pl.BlockSpec((tr, N), lambda i: (i, 0)),
        out_shape=jax.ShapeDtypeStruct((n, N), F32),
        compiler_params=_cparams("parallel"),
        name="dft_filter",
    )(a, f)


def _filter_spectrum(a, fe, fo):
    n, T = a.shape
    tr = _dft_tile_rows(T)
    o = _dft_raw(a, jnp.concatenate([fe, fo], axis=1))
    o = o.reshape(n // tr, 2, tr // 2, 2 * HY_WIDTH)
    cosp = o[:, 0].reshape(T, 2 * HY_WIDTH)
    sinp = o[:, 1].reshape(T, 2 * HY_WIDTH)
    hr = cosp[:, :HY_WIDTH]
    hi = -sinp[:, HY_WIDTH:]
    hnyq = sinp[0:1, :HY_WIDTH]
    first = (jnp.arange(T) == 0)[:, None]
    ck = jnp.where(first, 1.0 / n, 2.0 / n)
    t1 = ck * hr
    t2 = jnp.where(first, 0.0, ck * hi)
    t4 = jnp.where(first, hnyq / n, ck * hr)
    return t1, t2, t4


def _dft_fwd_kernel(a_ref, z_ref, t1_ref, t2_ref, t4_ref, y_ref):
    half = a_ref.shape[0] // 2
    u = jnp.dot(a_ref[...], z_ref[...], preferred_element_type=F32)
    re, im = u[:half], u[half:]
    t2 = t2_ref[...]
    y_ref[:half, :] = (re * t1_ref[...] + im * t2).astype(y_ref.dtype)
    y_ref[half:, :] = (im * t4_ref[...] - re * t2).astype(y_ref.dtype)


def _dft_fwd(a, zb, t1, t2, t4):
    n, T = a.shape
    B = zb.shape[0]
    tr = _dft_tile_rows(T)
    tmap = lambda i, b: (i, 0)
    return pl.pallas_call(
        _dft_fwd_kernel,
        grid=(n // tr, B),
        in_specs=[pl.BlockSpec((tr, T), tmap),
                  pl.BlockSpec((None, T, HY_WIDTH), lambda i, b: (b, 0, 0)),
                  pl.BlockSpec((tr // 2, HY_WIDTH), tmap),
                  pl.BlockSpec((tr // 2, HY_WIDTH), tmap),
                  pl.BlockSpec((tr // 2, HY_WIDTH), tmap)],
        out_specs=pl.BlockSpec((None, tr, HY_WIDTH), lambda i, b: (b, i, 0)),
        out_shape=jax.ShapeDtypeStruct((B, n, HY_WIDTH), BF16),
        compiler_params=_cparams("parallel", "arbitrary"),
        name="dft_forward",
    )(a, zb, t1, t2, t4)


def _dft_inv_kernel(at_ref, y_ref, x0_ref, z_ref, skip_ref, o_ref):
    y = jnp.dot(at_ref[...], y_ref[...], preferred_element_type=F32)
    o_ref[...] = x0_ref[...] * (y + z_ref[...] * skip_ref[...])


def _dft_inv(at, y, x0, z, skip):
    T, n = at.shape
    B = y.shape[0]
    tt = min(512, T)
    rmap = lambda i, b: (b, i, 0)
    return pl.pallas_call(
        _dft_inv_kernel,
        grid=(T // tt, B),
        in_specs=[pl.BlockSpec((tt, n), lambda i, b: (i, 0)),
                  pl.BlockSpec((None, n, HY_WIDTH), lambda i, b: (b, 0, 0)),
                  pl.BlockSpec((None, tt, HY_WIDTH), rmap),
                  pl.BlockSpec((None, tt, HY_WIDTH), rmap),
                  pl.BlockSpec((1, HY_WIDTH), lambda i, b: (0, 0))],
        out_specs=pl.BlockSpec((None, tt, HY_WIDTH), rmap),
        out_shape=jax.ShapeDtypeStruct((B, T, HY_WIDTH), F32),
        compiler_params=_cparams("parallel", "arbitrary"),
        name="dft_inverse",
    )(at, y, x0, z, skip.reshape(1, HY_WIDTH))


def _hyena(uh, dft, spectrum, short_w, short_b, skip):
    a, at = dft
    x0, z, zb = _hy_pre(uh, short_w, short_b)
    y = _dft_fwd(a, zb, *spectrum)
    return _dft_inv(at, y, x0, z, skip)


def _outproj_kernel(ya_ref, ogq_ref, gg_ref, ohy_ref, gh_ref, x_ref, gate_ref, w_ref,
                    gqg_ref, hyg_ref, fg_ref, o_ref, *, final):
    yg = _rms_rows(ogq_ref[...], gqg_ref[...]) * gg_ref[...].astype(F32)
    yh = _rms_rows(ohy_ref[...], hyg_ref[...]) * gh_ref[...].astype(F32)
    a0, a1, a2 = DA_WIDTH, DA_WIDTH + GQ_WIDTH, D_MIX
    y = jnp.dot(ya_ref[...], w_ref[0:a0, :], preferred_element_type=F32)
    y = y + jnp.dot(yg.astype(BF16), w_ref[a0:a1, :], preferred_element_type=F32)
    y = y + jnp.dot(yh.astype(BF16), w_ref[a1:a2, :], preferred_element_type=F32)
    out = x_ref[...] + gate_ref[...] * y
    if final:
        out = _rms_rows(out, fg_ref[...])
    o_ref[...] = out


def _outproj(ya, ogq, gg, ohy, gh, xt, gate, w_bf, gq_out_g, hy_out_g, final_g, final):
    B, T, D = xt.shape
    tm = min(512, T)
    row = lambda b, i: (b, i, 0)
    c2 = lambda b, i: (0, 0)
    return pl.pallas_call(
        functools.partial(_outproj_kernel, final=final),
        grid=(B, T // tm),
        in_specs=[pl.BlockSpec((None, tm, DA_WIDTH), row),
                  pl.BlockSpec((None, tm, GQ_WIDTH), row),
                  pl.BlockSpec((None, tm, GQ_WIDTH), row),
                  pl.BlockSpec((None, tm, HY_WIDTH), row),
                  pl.BlockSpec((None, tm, HY_WIDTH), row),
                  pl.BlockSpec((None, tm, D), row),
                  pl.BlockSpec((None, 1, D), lambda b, i: (b, 0, 0)),
                  pl.BlockSpec((D_MIX, D), c2),
                  pl.BlockSpec((1, GQ_WIDTH), c2),
                  pl.BlockSpec((1, HY_WIDTH), c2),
                  pl.BlockSpec((1, D), c2)],
        out_specs=pl.BlockSpec((None, tm, D), row),
        out_shape=jax.ShapeDtypeStruct((B, T, D), F32),
        compiler_params=_cparams("parallel", "parallel"),
        name="outproj_final" if final else "outproj",
    )(ya, ogq, gg, ohy, gh, xt, gate, w_bf, gq_out_g.reshape(1, GQ_WIDTH),
      hy_out_g.reshape(1, HY_WIDTH), final_g.reshape(1, D))


def kernel(x, c, ctx, c_ctx, ada_w, ada_b, norm_g, w_in, w_out, da_lambda, da_subln_g, gq_q_g, gq_k_g, gq_out_g, hy_short_w, hy_short_b, hy_w1, hy_b1, hy_w2, hy_b2, hy_w3, hy_b3, hy_w4, hy_freq, hy_bias, hy_out_g, final_g):
    B, L, D = x.shape
    Lc = ctx.shape[1]
    depth = ada_w.shape[0]
    assert L % GRID_W == 0 and D_IN == w_in.shape[2]

    tabs = _rope_tables(L, DA_QK_DIM, 2) + _rope_tables(L, GQ_DIM, 1)
    dft_lat = _dft_tables(L)
    dft_ctx = _dft_tables(Lc)

    n_mod = B + 1
    pad = (-n_mod) % 8
    c_all = jnp.concatenate([c, c_ctx[None], jnp.zeros((pad, D), F32)], axis=0)
    mod = _modulation(c_all, ada_w, ada_b)

    w_in_bf = w_in.astype(BF16)
    w_out_bf = w_out.astype(BF16)

    xc = ctx
    for l in range(depth):
        update_ctx = l < depth - 1
        lam_init = 0.8 - 0.6 * math.exp(-0.3 * l)
        shift, scale, gate = (mod[l, :B, i * D:(i + 1) * D][:, None, :] for i in range(3))
        shift_c, scale_c, gate_c = (jnp.broadcast_to(mod[l, B:B + 1, i * D:(i + 1) * D][:, None, :], (B, 1, D))
                                    for i in range(3))

        qa, ka, va, ga, qg, kg, vg, gg, uh, gh = _inproj(
            x, scale, shift, norm_g[l], w_in_bf[l], gq_q_g[l], gq_k_g[l], tabs)
        qa_c, ka_c, va_c, ga_c, qg_c, kg_c, vg_c, gg_c, uh_c, gh_c = _inproj(
            xc, scale_c, shift_c, norm_g[l], w_in_bf[l], gq_q_g[l], gq_k_g[l], None)

        cat = lambda a_c, a: jnp.concatenate([a_c, a], axis=1)
        ya = _da_attention(da_lambda[l], qa, cat(ka_c, ka), cat(va_c, va), ga, da_subln_g[l], lam_init)
        ogq = _gq_attention(qg, cat(kg_c, kg), cat(vg_c, vg))

        filt = (hy_w1[l], hy_b1[l], hy_w2[l], hy_b2[l], hy_w3[l], hy_b3[l], hy_w4[l], hy_freq[l])
        spec = _filter_spectrum(dft_lat[0], *_hyena_filter_eo(L, *filt))
        ohy = _hyena(uh, dft_lat, spec, hy_short_w[l], hy_short_b[l], hy_bias[l])

        if update_ctx:
            ya_c = _da_attention(da_lambda[l], qa_c, ka_c, va_c, ga_c, da_subln_g[l], lam_init)
            ogq_c = _gq_attention(qg_c, kg_c, vg_c)
            spec_c = _filter_spectrum(dft_ctx[0], *_hyena_filter_eo(Lc, *filt))
            ohy_c = _hyena(uh_c, dft_ctx, spec_c, hy_short_w[l], hy_short_b[l], hy_bias[l])
            xc = _outproj(ya_c, ogq_c, gg_c, ohy_c, gh_c, xc, gate_c, w_out_bf[l],
                          gq_out_g[l], hy_out_g[l], final_g, False)

        x = _outproj(ya, ogq, gg, ohy, gh, x, gate, w_out_bf[l],
                     gq_out_g[l], hy_out_g[l], final_g, l == depth - 1)
    return x
```

```python
import functools
import math

import jax
import jax.numpy as jnp
import numpy as np
from jax import lax
from jax.experimental import pallas as pl
from jax.experimental.pallas import tpu as pltpu

F32 = jnp.float32
BF16 = jnp.bfloat16

GRID_W = 64
EPS = 1e-6
ROPE_THETA = 10000.0
DA_HEADS = 4
DA_QK_DIM = 64
DA_V_DIM = 2 * DA_QK_DIM
DA_WIDTH = DA_HEADS * DA_V_DIM
GQ_HEADS = 8
GQ_KV_HEADS = 2
GQ_GROUP = GQ_HEADS // GQ_KV_HEADS
GQ_DIM = 128
GQ_WIDTH = GQ_HEADS * GQ_DIM
HY_WIDTH = 512
HY_EMB = 33
HY_BANDS = (HY_EMB - 1) // 2
HY_FFN = 64
HY_MAX_DECAY = math.log(1e-2) / 0.3
HY_MIN_DECAY = math.log(1e-2) / 1.5
HY_U = 3 * HY_WIDTH
D_MIX = DA_WIDTH + GQ_WIDTH + HY_WIDTH
COL_SIZES = (DA_WIDTH, DA_WIDTH, DA_WIDTH, DA_WIDTH,
             GQ_WIDTH, GQ_KV_HEADS * GQ_DIM, GQ_KV_HEADS * GQ_DIM, GQ_WIDTH,
             HY_U, HY_WIDTH)
COL_OFFS = tuple(int(v) for v in np.cumsum((0,) + COL_SIZES))
D_IN = COL_OFFS[-1]

LANE = 128
ATTN_CHAIN_ROWS = 256
ATTN_CHAINS = 16
ATTN_LOOKAHEAD = 1
DFT_FACTOR = 64
LOG2E = math.log2(math.e)
KEY_GROUP_ROWS = 256
VMEM_LIMIT = 56 * 1024 * 1024


def _cparams(*sem):
    return pltpu.CompilerParams(dimension_semantics=sem, vmem_limit_bytes=VMEM_LIMIT)


def _silu(g):
    return g / (1.0 + jnp.exp(-g))


def _rms_rows(x, g):
    return x * lax.rsqrt(jnp.mean(x * x, axis=-1, keepdims=True) + EPS) * g


def _mod_kernel(c_ref, w_ref, b_ref, o_ref):
    c = c_ref[...]
    o_ref[...] = jnp.dot(_silu(c), w_ref[...], preferred_element_type=F32,
                         precision=lax.Precision.HIGHEST) + b_ref[...]


def _modulation(c_all, ada_w, ada_b):
    depth, d, n3 = ada_w.shape
    r = c_all.shape[0]
    tn = 512
    return pl.pallas_call(
        _mod_kernel,
        grid=(depth, n3 // tn),
        in_specs=[pl.BlockSpec((r, d), lambda l, j: (0, 0)),
                  pl.BlockSpec((None, d, tn), lambda l, j: (l, 0, j)),
                  pl.BlockSpec((None, 1, tn), lambda l, j: (l, 0, j))],
        out_specs=pl.BlockSpec((None, r, tn), lambda l, j: (l, 0, j)),
        out_shape=jax.ShapeDtypeStruct((depth, r, n3), F32),
        compiler_params=_cparams("parallel", "parallel"),
        name="modulation",
    )(c_all, ada_w, ada_b.reshape(depth, 1, n3))


def _rope_tables(L, head_dim, reps):
    rows_n = L // GRID_W
    row = jnp.repeat(jnp.arange(rows_n), GRID_W).astype(F32)
    col = jnp.tile(jnp.arange(GRID_W), rows_n).astype(F32)
    axis_dim = head_dim // 2
    inv = ROPE_THETA ** (-jnp.arange(0, axis_dim, 2, dtype=F32) / axis_dim)
    ar, ac = row[:, None] * inv[None], col[:, None] * inv[None]
    z = jnp.zeros_like(ar)
    cos = jnp.concatenate([jnp.cos(ar), jnp.cos(ar), jnp.cos(ac), jnp.cos(ac)], axis=-1)
    s_lo = jnp.concatenate([-jnp.sin(ar), z, -jnp.sin(ac), z], axis=-1)
    s_hi = jnp.concatenate([z, jnp.sin(ar), z, jnp.sin(ac)], axis=-1)
    return tuple(jnp.tile(t, (1, reps)) for t in (cos, s_lo, s_hi))


def _rope_block(x, cos, s_lo, s_hi, half):
    fwd = pltpu.roll(x, LANE - half, axis=1)
    bwd = pltpu.roll(x, half, axis=1)
    return x * cos + fwd * s_lo + bwd * s_hi


def _inproj_kernel(*refs, rope):
    x_ref, sc_ref, sh_ref, ng_ref, w_ref, qg_g_ref, kg_g_ref = refs[:7]
    if rope:
        tabs = refs[7:13]
        outs = refs[13:]
    else:
        tabs = None
        outs = refs[7:]
    qa_o, ka_o, va_o, ga_o, qg_o, kg_o, vg_o, gg_o, uh_o, gh_o = outs

    x = x_ref[...]
    h = _rms_rows(x, ng_ref[...])
    h = h * (1.0 + sc_ref[...]) + sh_ref[...]
    hb = h.astype(BF16)

    def proj(group):
        off = COL_OFFS[group]
        return jnp.dot(hb, w_ref[:, off:off + COL_SIZES[group]], preferred_element_type=F32)

    def blocks(v):
        return [(slice(b * LANE, (b + 1) * LANE), v[:, b * LANE:(b + 1) * LANE])
                for b in range(v.shape[1] // LANE)]

    def rope_da(v):
        if not rope:
            return v
        return _rope_block(v, tabs[0][...], tabs[1][...], tabs[2][...], DA_QK_DIM // 4)

    def rope_gq(v):
        if not rope:
            return v
        return _rope_block(v, tabs[3][...], tabs[4][...], tabs[5][...], GQ_DIM // 4)

    for sl, v in blocks(proj(0)):
        qa_o[:, sl] = (rope_da(v) * (LOG2E * DA_QK_DIM ** -0.5)).astype(qa_o.dtype)
    for sl, v in blocks(proj(1)):
        ka_o[:, sl] = rope_da(v).astype(ka_o.dtype)
    va_o[...] = proj(2).astype(va_o.dtype)
    ga_o[...] = _silu(proj(3)).astype(ga_o.dtype)
    for sl, v in blocks(proj(4)):
        qg_o[:, sl] = (rope_gq(_rms_rows(v, qg_g_ref[...])) * (LOG2E * GQ_DIM ** -0.5)).astype(qg_o.dtype)
    for sl, v in blocks(proj(5)):
        kg_o[:, sl] = rope_gq(_rms_rows(v, kg_g_ref[...])).astype(kg_o.dtype)
    vg_o[...] = proj(6).astype(vg_o.dtype)
    gg_o[...] = _silu(proj(7)).astype(gg_o.dtype)
    uh_o[...] = proj(8)
    gh_o[...] = _silu(proj(9)).astype(gh_o.dtype)


def _inproj(xt, scale, shift, norm_g, w_bf, q_g, k_g, tabs):
    B, T, D = xt.shape
    tm = min(512, T)
    rope = tabs is not None
    row = lambda b, i: (b, i, 0)
    const2 = lambda b, i: (0, 0)
    in_specs = [pl.BlockSpec((None, tm, D), row),
                pl.BlockSpec((None, 1, D), lambda b, i: (b, 0, 0)),
                pl.BlockSpec((None, 1, D), lambda b, i: (b, 0, 0)),
                pl.BlockSpec((1, D), const2),
                pl.BlockSpec((D, D_IN), const2, pipeline_mode=pl.Buffered(1)),
                pl.BlockSpec((1, GQ_DIM), const2),
                pl.BlockSpec((1, GQ_DIM), const2)]
    args = [xt, scale, shift, norm_g.reshape(1, D), w_bf, q_g.reshape(1, GQ_DIM), k_g.reshape(1, GQ_DIM)]
    if rope:
        in_specs += [pl.BlockSpec((tm, LANE), lambda b, i: (i, 0))] * 6
        args += list(tabs)
    widths = (DA_WIDTH, DA_WIDTH, DA_WIDTH, DA_WIDTH, GQ_WIDTH, GQ_KV_HEADS * GQ_DIM,
              GQ_KV_HEADS * GQ_DIM, GQ_WIDTH, HY_U, HY_WIDTH)
    dtypes = (BF16, BF16, BF16, BF16, BF16, BF16, BF16, BF16, F32, BF16)
    return pl.pallas_call(
        functools.partial(_inproj_kernel, rope=rope),
        grid=(B, T // tm),
        in_specs=in_specs,
        out_specs=[pl.BlockSpec((None, tm, w), row) for w in widths],
        out_shape=[jax.ShapeDtypeStruct((B, T, w), dt) for w, dt in zip(widths, dtypes)],
        compiler_params=_cparams("parallel", "parallel"),
        name="inproj_rope" if rope else "inproj_ctx",
    )(*args)


def _reduce_keys(a, op):
    tk, m = a.shape
    g = KEY_GROUP_ROWS if tk % KEY_GROUP_ROWS == 0 else tk
    part = op(a.reshape(tk // g, g, m), axis=0)
    return op(part, axis=0, keepdims=True)


def _scores_t(qs, k):
    return lax.dot_general(k, qs, (((1,), (1,)), ((), ())), preferred_element_type=F32)


def _softmax_pv_t(st, vt):
    m = _reduce_keys(st, jnp.max)
    p = jnp.exp2(st - m)
    l = _reduce_keys(p, jnp.sum)
    ot = jnp.dot(vt, p.astype(BF16), preferred_element_type=F32)
    return (ot / l).T


def _attend_chains(queries, k_ref, vt_ref):
    n = len(queries)
    scores = [_scores_t(queries[j](), k_ref[...]) for j in range(min(ATTN_LOOKAHEAD, n))]
    outs = []
    for i in range(n):
        if i + ATTN_LOOKAHEAD < n:
            scores.append(_scores_t(queries[i + ATTN_LOOKAHEAD](), k_ref[...]))
        outs.append(_softmax_pv_t(scores[i], vt_ref[...]))
        scores[i] = None
    return outs


def _da_kernel(lam_ref, q_ref, k_ref, v_ref, gate_ref, sg_ref, o_ref, *, lam_init):
    tq = q_ref.shape[0]
    lm = lam_ref[...]
    lam = (jnp.exp(jnp.sum(lm[0:1] * lm[1:2], axis=-1, keepdims=True))
           - jnp.exp(jnp.sum(lm[2:3] * lm[3:4], axis=-1, keepdims=True)) + lam_init)
    ch = min(ATTN_CHAIN_ROWS, tq)

    def component(r, c):
        def load():
            q = q_ref[r * ch:(r + 1) * ch, :]
            lane = lax.broadcasted_iota(jnp.int32, q.shape, 1)
            keep = (lane < DA_QK_DIM) if c == 0 else (lane >= DA_QK_DIM)
            return jnp.where(keep, q, jnp.zeros_like(q))
        return load

    n = tq // ch
    outs = _attend_chains([component(r, c) for r in range(n) for c in range(2)], k_ref, v_ref)
    for r in range(n):
        rows = slice(r * ch, (r + 1) * ch)
        o = _rms_rows(outs[2 * r] - lam * outs[2 * r + 1], sg_ref[...]) * (1.0 - lam_init)
        o_ref[rows, :] = (o * gate_ref[rows, :].astype(F32)).astype(o_ref.dtype)


def _da_attention(lam_p, q, k, v, gate, subln_g, lam_init):
    B, Tq, _ = q.shape
    Tk = k.shape[1]
    tq = min(ATTN_CHAINS * ATTN_CHAIN_ROWS // 2, Tq)
    qmap = lambda b, h, i: (b, i, h)
    kmap = lambda b, h, i: (b, 0, h)
    vmap = lambda b, h, i: (b, h, 0)
    return pl.pallas_call(
        functools.partial(_da_kernel, lam_init=lam_init),
        grid=(B, DA_HEADS, Tq // tq),
        in_specs=[pl.BlockSpec((4, DA_QK_DIM), lambda b, h, i: (0, 0)),
                  pl.BlockSpec((None, tq, LANE), qmap),
                  pl.BlockSpec((None, Tk, LANE), kmap),
                  pl.BlockSpec((None, LANE, Tk), vmap),
                  pl.BlockSpec((None, tq, LANE), qmap),
                  pl.BlockSpec((1, LANE), lambda b, h, i: (0, 0))],
        out_specs=pl.BlockSpec((None, tq, LANE), qmap),
        out_shape=jax.ShapeDtypeStruct((B, Tq, DA_WIDTH), BF16),
        compiler_params=_cparams("parallel", "parallel", "arbitrary"),
        name="diff_attention",
    )(lam_p, q, k, v, gate, subln_g.reshape(1, DA_V_DIM))


def _gq_kernel(q_ref, k_ref, v_ref, o_ref):
    tq = q_ref.shape[0]
    ch = min(ATTN_CHAIN_ROWS, tq)
    where = [(slice(c * ch, (c + 1) * ch), slice(r * LANE, (r + 1) * LANE))
             for c in range(tq // ch) for r in range(GQ_GROUP)]
    outs = _attend_chains([functools.partial(lambda w: q_ref[w[0], w[1]], w) for w in where], k_ref, v_ref)
    for (rows, sl), o in zip(where, outs):
        o_ref[rows, sl] = o.astype(o_ref.dtype)


def _gq_attention(q, k, v):
    B, Tq, _ = q.shape
    Tk = k.shape[1]
    tq = min(ATTN_CHAINS * ATTN_CHAIN_ROWS // GQ_GROUP, Tq)
    gw = GQ_GROUP * GQ_DIM
    qmap = lambda b, g, i: (b, i, g)
    kmap = lambda b, g, i: (b, 0, g)
    vmap = lambda b, g, i: (b, g, 0)
    return pl.pallas_call(
        _gq_kernel,
        grid=(B, GQ_KV_HEADS, Tq // tq),
        in_specs=[pl.BlockSpec((None, tq, gw), qmap),
                  pl.BlockSpec((None, Tk, LANE), kmap),
                  pl.BlockSpec((None, LANE, Tk), vmap)],
        out_specs=pl.BlockSpec((None, tq, gw), qmap),
        out_shape=jax.ShapeDtypeStruct((B, Tq, GQ_WIDTH), F32),
        compiler_params=_cparams("parallel", "parallel", "arbitrary"),
        name="gqa_attention",
    )(q, k, v)


def _hy_pre_kernel(u0_ref, u1_ref, u2_ref, w0_ref, w1_ref, w2_ref, b0_ref, b1_ref, b2_ref,
                   x0_ref, z_ref, zb_ref):
    T = u0_ref.shape[0]
    row = lax.broadcasted_iota(jnp.int32, u0_ref.shape, 0)

    def sconv(u_ref, w_ref, b_ref):
        u = u_ref[...]
        w = w_ref[...]
        prev = jnp.where(row == 0, 0.0, pltpu.roll(u, 1, axis=0))
        nxt = jnp.where(row == T - 1, 0.0, pltpu.roll(u, T - 1, axis=0))
        return prev * w[0:1] + u * w[1:2] + nxt * w[2:3] + b_ref[...]

    x0_ref[...] = sconv(u0_ref, w0_ref, b0_ref)
    z = sconv(u2_ref, w2_ref, b2_ref) * sconv(u1_ref, w1_ref, b1_ref)
    z_ref[...] = z
    zb_ref[...] = z.astype(BF16)


def _hy_pre(uh, short_w, short_b):
    B, T, _ = uh.shape
    nb = HY_WIDTH // LANE
    u_specs = [pl.BlockSpec((None, T, LANE), lambda b, j, p=p: (b, 0, p * nb + j)) for p in range(3)]
    w_specs = [pl.BlockSpec((3, LANE), lambda b, j, p=p: (0, p * nb + j)) for p in range(3)]
    b_specs = [pl.BlockSpec((1, LANE), lambda b, j, p=p: (0, p * nb + j)) for p in range(3)]
    out_spec = pl.BlockSpec((None, T, LANE), lambda b, j: (b, 0, j))
    sb = short_b.reshape(1, HY_U)
    return pl.pallas_call(
        _hy_pre_kernel,
        grid=(B, nb),
        in_specs=u_specs + w_specs + b_specs,
        out_specs=[out_spec] * 3,
        out_shape=[jax.ShapeDtypeStruct((B, T, HY_WIDTH), F32),
                   jax.ShapeDtypeStruct((B, T, HY_WIDTH), F32),
                   jax.ShapeDtypeStruct((B, T, HY_WIDTH), BF16)],
        compiler_params=_cparams("parallel", "parallel"),
        name="hyena_short_conv",
    )(uh, uh, uh, short_w, short_w, short_w, sb, sb, sb)


def _filter_kernel(z_ref, w1_ref, b1_ref, w2_ref, b2_ref, w3_ref, b3_ref, w4f_ref, w4b_ref,
                   fr_ref, t_ref, dl_ref, fe_ref, fo_ref, h_ref):
    hp = lax.Precision.HIGHEST

    @pl.when(pl.program_id(0) == 0)
    def _():
        fr = fr_ref[...]
        h = jnp.sin(fr * (jnp.dot(z_ref[...], w1_ref[...], preferred_element_type=F32, precision=hp) + b1_ref[...]))
        h = jnp.sin(fr * (jnp.dot(h, w2_ref[...], preferred_element_type=F32, precision=hp) + b2_ref[...]))
        h_ref[...] = jnp.sin(fr * (jnp.dot(h, w3_ref[...], preferred_element_type=F32, precision=hp) + b3_ref[...]))

    h = h_ref[...]
    decay = jnp.exp(-t_ref[...] * jnp.abs(dl_ref[...]))
    fwd = jnp.dot(h, w4f_ref[...], preferred_element_type=F32, precision=hp) * decay
    bwd = jnp.dot(h, w4b_ref[...], preferred_element_type=F32, precision=hp) * decay
    row = lax.broadcasted_iota(jnp.int32, bwd.shape, 0)
    bwd = jnp.where(row == 0, 0.0, bwd)
    ss = jnp.sum(fwd * fwd + bwd * bwd, axis=0, keepdims=True)
    sc = lax.rsqrt(ss + EPS)
    fe_ref[...] = ((fwd + bwd) * sc).astype(fe_ref.dtype)
    fo_ref[...] = ((fwd - bwd) * sc).astype(fo_ref.dtype)


def _hyena_filter_eo(T, w1, b1, w2, b2, w3, b3, w4, freq):
    t = jnp.linspace(0.0, 1.0, T, dtype=F32)[:, None]
    w = 2.0 * math.pi * jnp.arange(T, dtype=F32)[:, None] / T
    f = jnp.linspace(1e-4, HY_BANDS - 1, HY_BANDS, dtype=F32)[None]
    z = jnp.concatenate([t, jnp.cos(f * w), -jnp.sin(f * w)], axis=-1)
    z = jnp.pad(z, ((0, 0), (0, LANE - HY_EMB)))
    w1p = jnp.pad(w1, ((0, LANE - HY_EMB), (0, 0)))
    deltas = jnp.linspace(HY_MIN_DECAY, HY_MAX_DECAY, HY_WIDTH, dtype=F32)[None]
    nb = HY_WIDTH // LANE
    c2 = lambda j: (0, 0)
    vec = lambda a: a.reshape(1, HY_FFN)
    return pl.pallas_call(
        _filter_kernel,
        grid=(nb,),
        in_specs=[pl.BlockSpec((T, LANE), c2),
                  pl.BlockSpec((LANE, HY_FFN), c2), pl.BlockSpec((1, HY_FFN), c2),
                  pl.BlockSpec((HY_FFN, HY_FFN), c2), pl.BlockSpec((1, HY_FFN), c2),
                  pl.BlockSpec((HY_FFN, HY_FFN), c2), pl.BlockSpec((1, HY_FFN), c2),
                  pl.BlockSpec((HY_FFN, LANE), lambda j: (0, j)),
                  pl.BlockSpec((HY_FFN, LANE), lambda j: (0, nb + j)),
                  pl.BlockSpec((1, HY_FFN), c2),
                  pl.BlockSpec((T, 1), c2),
                  pl.BlockSpec((1, LANE), lambda j: (0, j))],
        out_specs=[pl.BlockSpec((T, LANE), lambda j: (0, j))] * 2,
        out_shape=[jax.ShapeDtypeStruct((T, HY_WIDTH), BF16)] * 2,
        scratch_shapes=[pltpu.VMEM((T, HY_FFN), F32)],
        compiler_params=_cparams("arbitrary"),
        name="hyena_filter",
    )(z, w1p, vec(b1), w2, vec(b2), w3, vec(b3), w4, w4, vec(freq), t, deltas)


def _dft_tile_rows(T):
    return min(1024, 2 * T)


def _dft_tables(T):
    n = 2 * T
    tr = _dft_tile_rows(T)
    half = tr // 2
    f = DFT_FACTOR
    assert T % f == 0 and half % f == 0
    ang = 2.0 * math.pi / n

    def cs(k, s):
        a = ((k * s) % n).astype(F32) * ang
        return jnp.cos(a), jnp.sin(a)

    s = jnp.arange(T, dtype=jnp.int32)[None]
    cl, sl = cs(jnp.arange(f, dtype=jnp.int32)[:, None], s)
    ch, sh = cs(jnp.arange(T // f, dtype=jnp.int32)[:, None] * f, s)
    x = jnp.stack([cl, sl])[None, :, None]
    w = jnp.stack([-sl, cl])[None, :, None]
    ch5 = ch.reshape(n // tr, 1, half // f, 1, T)
    sh5 = sh.reshape(n // tr, 1, half // f, 1, T)
    a = (ch5 * x + sh5 * w).reshape(n, T).astype(BF16)
    nyq = jnp.where(jnp.arange(T) % 2 == 0, 1.0, -1.0).astype(BF16)
    a = a.at[half].set(nyq)

    r = np.arange(n)
    local = r % tr
    k = (r // tr) * half + local % half
    is_sin = local >= half
    is_nyq = is_sin & (k == 0)
    k = jnp.asarray(np.where(is_nyq, T, k), jnp.int32)[None]
    sel = jnp.asarray(is_sin & ~is_nyq)[None]
    c_lo, s_lo = cs(k, jnp.arange(f, dtype=jnp.int32)[:, None])
    c_hi, s_hi = cs(k, jnp.arange(T // f, dtype=jnp.int32)[:, None] * f)
    x_lo = jnp.where(sel, s_lo, c_lo)
    w_lo = jnp.where(sel, c_lo, -s_lo)
    at = (x_lo[None] * c_hi[:, None] + w_lo[None] * s_hi[:, None]).reshape(T, n).astype(BF16)
    return a, at


def _dft_raw_kernel(a_ref, z_ref, o_ref):
    o_ref[...] = jnp.dot(a_ref[...], z_ref[...], preferred_element_type=F32)


def _dft_raw(a, f):
    n, T = a.shape
    N = f.shape[1]
    tr = _dft_tile_rows(T)
    return pl.pallas_call(
        _dft_raw_kernel,
        grid=(n // tr,),
        in_specs=[pl.BlockSpec((tr, T), lambda i: (i, 0)),
                  pl.BlockSpec((T, N), lambda i: (0, 0))],
        out_specs=pl.BlockSpec((tr, N), lambda i: (i, 0)),
        out_shape=jax.ShapeDtypeStruct((n, N), F32),
        compiler_params=_cparams("parallel"),
        name="dft_filter",
    )(a, f)


def _filter_spectrum(a, fe, fo):
    n, T = a.shape
    tr = _dft_tile_rows(T)
    o = _dft_raw(a, jnp.concatenate([fe, fo], axis=1))
    o = o.reshape(n // tr, 2, tr // 2, 2 * HY_WIDTH)
    cosp = o[:, 0].reshape(T, 2 * HY_WIDTH)
    sinp = o[:, 1].reshape(T, 2 * HY_WIDTH)
    hr = cosp[:, :HY_WIDTH]
    hi = -sinp[:, HY_WIDTH:]
    hnyq = sinp[0:1, :HY_WIDTH]
    first = (jnp.arange(T) == 0)[:, None]
    ck = jnp.where(first, 1.0 / n, 2.0 / n)
    t1 = ck * hr
    t2 = jnp.where(first, 0.0, ck * hi)
    t4 = jnp.where(first, hnyq / n, ck * hr)
    return t1, t2, t4


def _dft_fwd_kernel(a_ref, z_ref, t1_ref, t2_ref, t4_ref, y_ref):
    half = a_ref.shape[0] // 2
    u = jnp.dot(a_ref[...], z_ref[...], preferred_element_type=F32)
    re, im = u[:half], u[half:]
    t2 = t2_ref[...]
    y_ref[:half, :] = (re * t1_ref[...] + im * t2).astype(y_ref.dtype)
    y_ref[half:, :] = (im * t4_ref[...] - re * t2).astype(y_ref.dtype)


def _dft_fwd(a, zb, t1, t2, t4):
    n, T = a.shape
    B = zb.shape[0]
    tr = _dft_tile_rows(T)
    tmap = lambda i, b: (i, 0)
    return pl.pallas_call(
        _dft_fwd_kernel,
        grid=(n // tr, B),
        in_specs=[pl.BlockSpec((tr, T), tmap),
                  pl.BlockSpec((None, T, HY_WIDTH), lambda i, b: (b, 0, 0)),
                  pl.BlockSpec((tr // 2, HY_WIDTH), tmap),
                  pl.BlockSpec((tr // 2, HY_WIDTH), tmap),
                  pl.BlockSpec((tr // 2, HY_WIDTH), tmap)],
        out_specs=pl.BlockSpec((None, tr, HY_WIDTH), lambda i, b: (b, i, 0)),
        out_shape=jax.ShapeDtypeStruct((B, n, HY_WIDTH), BF16),
        compiler_params=_cparams("parallel", "arbitrary"),
        name="dft_forward",
    )(a, zb, t1, t2, t4)


def _dft_inv_kernel(at_ref, y_ref, x0_ref, z_ref, skip_ref, o_ref):
    y = jnp.dot(at_ref[...], y_ref[...], preferred_element_type=F32)
    o_ref[...] = x0_ref[...] * (y + z_ref[...] * skip_ref[...])


def _dft_inv(at, y, x0, z, skip):
    T, n = at.shape
    B = y.shape[0]
    tt = min(512, T)
    rmap = lambda i, b: (b, i, 0)
    return pl.pallas_call(
        _dft_inv_kernel,
        grid=(T // tt, B),
        in_specs=[pl.BlockSpec((tt, n), lambda i, b: (i, 0)),
                  pl.BlockSpec((None, n, HY_WIDTH), lambda i, b: (b, 0, 0)),
                  pl.BlockSpec((None, tt, HY_WIDTH), rmap),
                  pl.BlockSpec((None, tt, HY_WIDTH), rmap),
                  pl.BlockSpec((1, HY_WIDTH), lambda i, b: (0, 0))],
        out_specs=pl.BlockSpec((None, tt, HY_WIDTH), rmap),
        out_shape=jax.ShapeDtypeStruct((B, T, HY_WIDTH), F32),
        compiler_params=_cparams("parallel", "arbitrary"),
        name="dft_inverse",
    )(at, y, x0, z, skip.reshape(1, HY_WIDTH))


def _hyena(uh, dft, spectrum, short_w, short_b, skip):
    a, at = dft
    x0, z, zb = _hy_pre(uh, short_w, short_b)
    y = _dft_fwd(a, zb, *spectrum)
    return _dft_inv(at, y, x0, z, skip)


def _outproj_kernel(ya_ref, ogq_ref, gg_ref, ohy_ref, gh_ref, x_ref, gate_ref, w_ref,
                    gqg_ref, hyg_ref, fg_ref, o_ref, *, final):
    yg = _rms_rows(ogq_ref[...], gqg_ref[...]) * gg_ref[...].astype(F32)
    yh = _rms_rows(ohy_ref[...], hyg_ref[...]) * gh_ref[...].astype(F32)
    a0, a1, a2 = DA_WIDTH, DA_WIDTH + GQ_WIDTH, D_MIX
    y = jnp.dot(ya_ref[...], w_ref[0:a0, :], preferred_element_type=F32)
    y = y + jnp.dot(yg.astype(BF16), w_ref[a0:a1, :], preferred_element_type=F32)
    y = y + jnp.dot(yh.astype(BF16), w_ref[a1:a2, :], preferred_element_type=F32)
    out = x_ref[...] + gate_ref[...] * y
    if final:
        out = _rms_rows(out, fg_ref[...])
    o_ref[...] = out


def _outproj(ya, ogq, gg, ohy, gh, xt, gate, w_bf, gq_out_g, hy_out_g, final_g, final):
    B, T, D = xt.shape
    tm = min(512, T)
    row = lambda b, i: (b, i, 0)
    c2 = lambda b, i: (0, 0)
    return pl.pallas_call(
        functools.partial(_outproj_kernel, final=final),
        grid=(B, T // tm),
        in_specs=[pl.BlockSpec((None, tm, DA_WIDTH), row),
                  pl.BlockSpec((None, tm, GQ_WIDTH), row),
                  pl.BlockSpec((None, tm, GQ_WIDTH), row),
                  pl.BlockSpec((None, tm, HY_WIDTH), row),
                  pl.BlockSpec((None, tm, HY_WIDTH), row),
                  pl.BlockSpec((None, tm, D), row),
                  pl.BlockSpec((None, 1, D), lambda b, i: (b, 0, 0)),
                  pl.BlockSpec((D_MIX, D), c2),
                  pl.BlockSpec((1, GQ_WIDTH), c2),
                  pl.BlockSpec((1, HY_WIDTH), c2),
                  pl.BlockSpec((1, D), c2)],
        out_specs=pl.BlockSpec((None, tm, D), row),
        out_shape=jax.ShapeDtypeStruct((B, T, D), F32),
        compiler_params=_cparams("parallel", "parallel"),
        name="outproj_final" if final else "outproj",
    )(ya, ogq, gg, ohy, gh, xt, gate, w_bf, gq_out_g.reshape(1, GQ_WIDTH),
      hy_out_g.reshape(1, HY_WIDTH), final_g.reshape(1, D))


def kernel(x, c, ctx, c_ctx, ada_w, ada_b, norm_g, w_in, w_out, da_lambda, da_subln_g, gq_q_g, gq_k_g, gq_out_g, hy_short_w, hy_short_b, hy_w1, hy_b1, hy_w2, hy_b2, hy_w3, hy_b3, hy_w4, hy_freq, hy_bias, hy_out_g, final_g):
    B, L, D = x.shape
    Lc = ctx.shape[1]
    depth = ada_w.shape[0]
    assert L % GRID_W == 0 and D_IN == w_in.shape[2]

    tabs = _rope_tables(L, DA_QK_DIM, 2) + _rope_tables(L, GQ_DIM, 1)
    dft_lat = _dft_tables(L)
    dft_ctx = _dft_tables(Lc)

    n_mod = B + 1
    pad = (-n_mod) % 8
    c_all = jnp.concatenate([c, c_ctx[None], jnp.zeros((pad, D), F32)], axis=0)
    mod = _modulation(c_all, ada_w, ada_b)

    w_in_bf = w_in.astype(BF16)
    w_out_bf = w_out.astype(BF16)

    xc = ctx
    for l in range(depth):
        update_ctx = l < depth - 1
        lam_init = 0.8 - 0.6 * math.exp(-0.3 * l)
        shift, scale, gate = (mod[l, :B, i * D:(i + 1) * D][:, None, :] for i in range(3))
        shift_c, scale_c, gate_c = (jnp.broadcast_to(mod[l, B:B + 1, i * D:(i + 1) * D][:, None, :], (B, 1, D))
                                    for i in range(3))

        qa, ka, va, ga, qg, kg, vg, gg, uh, gh = _inproj(
            x, scale, shift, norm_g[l], w_in_bf[l], gq_q_g[l], gq_k_g[l], tabs)
        qa_c, ka_c, va_c, ga_c, qg_c, kg_c, vg_c, gg_c, uh_c, gh_c = _inproj(
            xc, scale_c, shift_c, norm_g[l], w_in_bf[l], gq_q_g[l], gq_k_g[l], None)

        cat = lambda a_c, a: jnp.concatenate([a_c, a], axis=1)
        tr = lambda a: jnp.swapaxes(a, 1, 2)
        ya = _da_attention(da_lambda[l], qa, cat(ka_c, ka), tr(cat(va_c, va)), ga, da_subln_g[l], lam_init)
        ogq = _gq_attention(qg, cat(kg_c, kg), tr(cat(vg_c, vg)))

        filt = (hy_w1[l], hy_b1[l], hy_w2[l], hy_b2[l], hy_w3[l], hy_b3[l], hy_w4[l], hy_freq[l])
        spec = _filter_spectrum(dft_lat[0], *_hyena_filter_eo(L, *filt))
        ohy = _hyena(uh, dft_lat, spec, hy_short_w[l], hy_short_b[l], hy_bias[l])

        if update_ctx:
            ya_c = _da_attention(da_lambda[l], qa_c, ka_c, tr(va_c), ga_c, da_subln_g[l], lam_init)
            ogq_c = _gq_attention(qg_c, kg_c, tr(vg_c))
            spec_c = _filter_spectrum(dft_ctx[0], *_hyena_filter_eo(Lc, *filt))
            ohy_c = _hyena(uh_c, dft_ctx, spec_c, hy_short_w[l], hy_short_b[l], hy_bias[l])
            xc = _outproj(ya_c, ogq_c, gg_c, ohy_c, gh_c, xc, gate_c, w_out_bf[l],
                          gq_out_g[l], hy_out_g[l], final_g, False)

        x = _outproj(ya, ogq, gg, ohy, gh, x, gate, w_out_bf[l],
                     gq_out_g[l], hy_out_g[l], final_g, l == depth - 1)
    return x
```

```python
import functools
import math

import jax
import jax.numpy as jnp
import numpy as np
from jax import lax
from jax.experimental import pallas as pl
from jax.experimental.pallas import tpu as pltpu

F32 = jnp.float32
BF16 = jnp.bfloat16

GRID_W = 64
EPS = 1e-6
ROPE_THETA = 10000.0
DA_HEADS = 4
DA_QK_DIM = 64
DA_V_DIM = 2 * DA_QK_DIM
DA_WIDTH = DA_HEADS * DA_V_DIM
GQ_HEADS = 8
GQ_KV_HEADS = 2
GQ_GROUP = GQ_HEADS // GQ_KV_HEADS
GQ_DIM = 128
GQ_WIDTH = GQ_HEADS * GQ_DIM
HY_WIDTH = 512
HY_EMB = 33
HY_BANDS = (HY_EMB - 1) // 2
HY_FFN = 64
HY_MAX_DECAY = math.log(1e-2) / 0.3
HY_MIN_DECAY = math.log(1e-2) / 1.5
HY_U = 3 * HY_WIDTH
D_MIX = DA_WIDTH + GQ_WIDTH + HY_WIDTH
COL_SIZES = (DA_WIDTH, DA_WIDTH, DA_WIDTH, DA_WIDTH,
             GQ_WIDTH, GQ_KV_HEADS * GQ_DIM, GQ_KV_HEADS * GQ_DIM, GQ_WIDTH,
             HY_U, HY_WIDTH)
COL_OFFS = tuple(int(v) for v in np.cumsum((0,) + COL_SIZES))
D_IN = COL_OFFS[-1]

LANE = 128
ATTN_CHAIN_ROWS = 256
ATTN_CHAINS = 16
ATTN_LOOKAHEAD = 1
DFT_FACTOR = 64
LOG2E = math.log2(math.e)
KEY_GROUP_ROWS = 256
VMEM_LIMIT = 56 * 1024 * 1024


def _cparams(*sem):
    return pltpu.CompilerParams(dimension_semantics=sem, vmem_limit_bytes=VMEM_LIMIT)


def _silu(g):
    return g / (1.0 + jnp.exp(-g))


def _rms_rows(x, g):
    return x * lax.rsqrt(jnp.mean(x * x, axis=-1, keepdims=True) + EPS) * g


def _mod_kernel(c_ref, w_ref, b_ref, o_ref):
    c = c_ref[...]
    o_ref[...] = jnp.dot(_silu(c), w_ref[...], preferred_element_type=F32,
                         precision=lax.Precision.HIGHEST) + b_ref[...]


def _modulation(c_all, ada_w, ada_b):
    depth, d, n3 = ada_w.shape
    r = c_all.shape[0]
    tn = 512
    return pl.pallas_call(
        _mod_kernel,
        grid=(depth, n3 // tn),
        in_specs=[pl.BlockSpec((r, d), lambda l, j: (0, 0)),
                  pl.BlockSpec((None, d, tn), lambda l, j: (l, 0, j)),
                  pl.BlockSpec((None, 1, tn), lambda l, j: (l, 0, j))],
        out_specs=pl.BlockSpec((None, r, tn), lambda l, j: (l, 0, j)),
        out_shape=jax.ShapeDtypeStruct((depth, r, n3), F32),
        compiler_params=_cparams("parallel", "parallel"),
        name="modulation",
    )(c_all, ada_w, ada_b.reshape(depth, 1, n3))


def _rope_tables(L, head_dim, reps):
    rows_n = L // GRID_W
    row = jnp.repeat(jnp.arange(rows_n), GRID_W).astype(F32)
    col = jnp.tile(jnp.arange(GRID_W), rows_n).astype(F32)
    axis_dim = head_dim // 2
    inv = ROPE_THETA ** (-jnp.arange(0, axis_dim, 2, dtype=F32) / axis_dim)
    ar, ac = row[:, None] * inv[None], col[:, None] * inv[None]
    z = jnp.zeros_like(ar)
    cos = jnp.concatenate([jnp.cos(ar), jnp.cos(ar), jnp.cos(ac), jnp.cos(ac)], axis=-1)
    s_lo = jnp.concatenate([-jnp.sin(ar), z, -jnp.sin(ac), z], axis=-1)
    s_hi = jnp.concatenate([z, jnp.sin(ar), z, jnp.sin(ac)], axis=-1)
    return tuple(jnp.tile(t, (1, reps)) for t in (cos, s_lo, s_hi))


def _rope_block(x, cos, s_lo, s_hi, half):
    fwd = pltpu.roll(x, LANE - half, axis=1)
    bwd = pltpu.roll(x, half, axis=1)
    return x * cos + fwd * s_lo + bwd * s_hi


def _inproj_kernel(*refs, rope, n_alias):
    x_ref, sc_ref, sh_ref, ng_ref, w_ref, qg_g_ref, kg_g_ref = refs[:7]
    n_in = 7 + (6 if rope else 0)
    tabs = refs[7:13] if rope else None
    qa_o, ka_o, va_o, ga_o, qg_o, kg_o, vg_o, gg_o, uh_o, gh_o = refs[n_in + n_alias:]

    x = x_ref[...]
    h = _rms_rows(x, ng_ref[...])
    h = h * (1.0 + sc_ref[...]) + sh_ref[...]
    hb = h.astype(BF16)

    def proj(group):
        off = COL_OFFS[group]
        return jnp.dot(hb, w_ref[:, off:off + COL_SIZES[group]], preferred_element_type=F32)

    def blocks(v):
        return [(slice(b * LANE, (b + 1) * LANE), v[:, b * LANE:(b + 1) * LANE])
                for b in range(v.shape[1] // LANE)]

    def rope_da(v):
        if not rope:
            return v
        return _rope_block(v, tabs[0][...], tabs[1][...], tabs[2][...], DA_QK_DIM // 4)

    def rope_gq(v):
        if not rope:
            return v
        return _rope_block(v, tabs[3][...], tabs[4][...], tabs[5][...], GQ_DIM // 4)

    for sl, v in blocks(proj(0)):
        qa_o[:, sl] = (rope_da(v) * (LOG2E * DA_QK_DIM ** -0.5)).astype(qa_o.dtype)
    for sl, v in blocks(proj(1)):
        ka_o[:, sl] = rope_da(v).astype(ka_o.dtype)
    va_o[...] = proj(2).T.astype(va_o.dtype)
    ga_o[...] = _silu(proj(3)).astype(ga_o.dtype)
    for sl, v in blocks(proj(4)):
        qg_o[:, sl] = (rope_gq(_rms_rows(v, qg_g_ref[...])) * (LOG2E * GQ_DIM ** -0.5)).astype(qg_o.dtype)
    for sl, v in blocks(proj(5)):
        kg_o[:, sl] = rope_gq(_rms_rows(v, kg_g_ref[...])).astype(kg_o.dtype)
    vg_o[...] = proj(6).T.astype(vg_o.dtype)
    gg_o[...] = _silu(proj(7)).astype(gg_o.dtype)
    uh_o[...] = proj(8)
    gh_o[...] = _silu(proj(9)).astype(gh_o.dtype)


def _inproj(xt, scale, shift, norm_g, w_bf, q_g, k_g, tabs, kv_rows, kv_row0, kv_bufs):
    B, T, D = xt.shape
    tm = min(512, T)
    assert kv_row0 % tm == 0
    rope = tabs is not None
    blk0 = kv_row0 // tm
    row = lambda b, i: (b, i, 0)
    krow = lambda b, i: (b, i + blk0, 0)
    vcol = lambda b, i: (b, 0, i + blk0)
    const2 = lambda b, i: (0, 0)
    in_specs = [pl.BlockSpec((None, tm, D), row),
                pl.BlockSpec((None, 1, D), lambda b, i: (b, 0, 0)),
                pl.BlockSpec((None, 1, D), lambda b, i: (b, 0, 0)),
                pl.BlockSpec((1, D), const2),
                pl.BlockSpec((D, D_IN), const2, pipeline_mode=pl.Buffered(1)),
                pl.BlockSpec((1, GQ_DIM), const2),
                pl.BlockSpec((1, GQ_DIM), const2)]
    args = [xt, scale, shift, norm_g.reshape(1, D), w_bf, q_g.reshape(1, GQ_DIM), k_g.reshape(1, GQ_DIM)]
    if rope:
        in_specs += [pl.BlockSpec((tm, LANE), lambda b, i: (i, 0))] * 6
        args += list(tabs)
    aliases = {}
    if kv_bufs is not None:
        kv_out_index = (1, 2, 5, 6)
        aliases = {len(args) + n: o for n, o in enumerate(kv_out_index)}
        in_specs += [pl.BlockSpec(memory_space=pl.ANY)] * len(kv_bufs)
        args += list(kv_bufs)
    kvw = GQ_KV_HEADS * GQ_DIM
    tok = lambda w, dt: (pl.BlockSpec((None, tm, w), row), jax.ShapeDtypeStruct((B, T, w), dt))
    keys = lambda w: (pl.BlockSpec((None, tm, w), krow), jax.ShapeDtypeStruct((B, kv_rows, w), BF16))
    vals = lambda w: (pl.BlockSpec((None, w, tm), vcol), jax.ShapeDtypeStruct((B, w, kv_rows), BF16))
    outs = [tok(DA_WIDTH, BF16), keys(DA_WIDTH), vals(DA_WIDTH), tok(DA_WIDTH, BF16),
            tok(GQ_WIDTH, BF16), keys(kvw), vals(kvw), tok(GQ_WIDTH, BF16),
            tok(HY_U, F32), tok(HY_WIDTH, BF16)]
    return pl.pallas_call(
        functools.partial(_inproj_kernel, rope=rope, n_alias=len(aliases)),
        grid=(B, T // tm),
        in_specs=in_specs,
        out_specs=[o[0] for o in outs],
        out_shape=[o[1] for o in outs],
        input_output_aliases=aliases,
        compiler_params=_cparams("parallel", "parallel"),
        name="inproj_rope" if rope else "inproj_ctx",
    )(*args)


def _reduce_keys(a, op):
    tk, m = a.shape
    g = KEY_GROUP_ROWS if tk % KEY_GROUP_ROWS == 0 else tk
    part = op(a.reshape(tk // g, g, m), axis=0)
    return op(part, axis=0, keepdims=True)


def _scores_t(qs, k):
    return lax.dot_general(k, qs, (((1,), (1,)), ((), ())), preferred_element_type=F32)


def _softmax_pv_t(st, vt):
    m = _reduce_keys(st, jnp.max)
    p = jnp.exp2(st - m)
    l = _reduce_keys(p, jnp.sum)
    ot = jnp.dot(vt, p.astype(BF16), preferred_element_type=F32)
    return (ot / l).T


def _attend_chains(queries, k_ref, vt_ref):
    n = len(queries)
    scores = [_scores_t(queries[j](), k_ref[...]) for j in range(min(ATTN_LOOKAHEAD, n))]
    outs = []
    for i in range(n):
        if i + ATTN_LOOKAHEAD < n:
            scores.append(_scores_t(queries[i + ATTN_LOOKAHEAD](), k_ref[...]))
        outs.append(_softmax_pv_t(scores[i], vt_ref[...]))
        scores[i] = None
    return outs


def _da_kernel(lam_ref, q_ref, k_ref, v_ref, gate_ref, sg_ref, o_ref, *, lam_init):
    tq = q_ref.shape[0]
    lm = lam_ref[...]
    lam = (jnp.exp(jnp.sum(lm[0:1] * lm[1:2], axis=-1, keepdims=True))
           - jnp.exp(jnp.sum(lm[2:3] * lm[3:4], axis=-1, keepdims=True)) + lam_init)
    ch = min(ATTN_CHAIN_ROWS, tq)

    def component(r, c):
        def load():
            q = q_ref[r * ch:(r + 1) * ch, :]
            lane = lax.broadcasted_iota(jnp.int32, q.shape, 1)
            keep = (lane < DA_QK_DIM) if c == 0 else (lane >= DA_QK_DIM)
            return jnp.where(keep, q, jnp.zeros_like(q))
        return load

    n = tq // ch
    outs = _attend_chains([component(r, c) for r in range(n) for c in range(2)], k_ref, v_ref)
    for r in range(n):
        rows = slice(r * ch, (r + 1) * ch)
        o = _rms_rows(outs[2 * r] - lam * outs[2 * r + 1], sg_ref[...]) * (1.0 - lam_init)
        o_ref[rows, :] = (o * gate_ref[rows, :].astype(F32)).astype(o_ref.dtype)


def _da_attention(lam_p, q, k, v, gate, subln_g, lam_init, Tk, kblk):
    B, Tq, _ = q.shape
    tq = min(ATTN_CHAINS * ATTN_CHAIN_ROWS // 2, Tq)
    qmap = lambda b, h, i: (b, i, h)
    kmap = lambda b, h, i: (b, kblk, h)
    vmap = lambda b, h, i: (b, h, kblk)
    return pl.pallas_call(
        functools.partial(_da_kernel, lam_init=lam_init),
        grid=(B, DA_HEADS, Tq // tq),
        in_specs=[pl.BlockSpec((4, DA_QK_DIM), lambda b, h, i: (0, 0)),
                  pl.BlockSpec((None, tq, LANE), qmap),
                  pl.BlockSpec((None, Tk, LANE), kmap),
                  pl.BlockSpec((None, LANE, Tk), vmap),
                  pl.BlockSpec((None, tq, LANE), qmap),
                  pl.BlockSpec((1, LANE), lambda b, h, i: (0, 0))],
        out_specs=pl.BlockSpec((None, tq, LANE), qmap),
        out_shape=jax.ShapeDtypeStruct((B, Tq, DA_WIDTH), BF16),
        compiler_params=_cparams("parallel", "parallel", "arbitrary"),
        name="diff_attention",
    )(lam_p, q, k, v, gate, subln_g.reshape(1, DA_V_DIM))


def _gq_kernel(q_ref, k_ref, v_ref, o_ref):
    tq = q_ref.shape[0]
    ch = min(ATTN_CHAIN_ROWS, tq)
    where = [(slice(c * ch, (c + 1) * ch), slice(r * LANE, (r + 1) * LANE))
             for c in range(tq // ch) for r in range(GQ_GROUP)]
    outs = _attend_chains([functools.partial(lambda w: q_ref[w[0], w[1]], w) for w in where], k_ref, v_ref)
    for (rows, sl), o in zip(where, outs):
        o_ref[rows, sl] = o.astype(o_ref.dtype)


def _gq_attention(q, k, v, Tk, kblk):
    B, Tq, _ = q.shape
    tq = min(ATTN_CHAINS * ATTN_CHAIN_ROWS // GQ_GROUP, Tq)
    gw = GQ_GROUP * GQ_DIM
    qmap = lambda b, g, i: (b, i, g)
    kmap = lambda b, g, i: (b, kblk, g)
    vmap = lambda b, g, i: (b, g, kblk)
    return pl.pallas_call(
        _gq_kernel,
        grid=(B, GQ_KV_HEADS, Tq // tq),
        in_specs=[pl.BlockSpec((None, tq, gw), qmap),
                  pl.BlockSpec((None, Tk, LANE), kmap),
                  pl.BlockSpec((None, LANE, Tk), vmap)],
        out_specs=pl.BlockSpec((None, tq, gw), qmap),
        out_shape=jax.ShapeDtypeStruct((B, Tq, GQ_WIDTH), F32),
        compiler_params=_cparams("parallel", "parallel", "arbitrary"),
        name="gqa_attention",
    )(q, k, v)


def _hy_pre_kernel(u0_ref, u1_ref, u2_ref, w0_ref, w1_ref, w2_ref, b0_ref, b1_ref, b2_ref,
                   x0_ref, z_ref, zb_ref):
    T = u0_ref.shape[0]
    row = lax.broadcasted_iota(jnp.int32, u0_ref.shape, 0)

    def sconv(u_ref, w_ref, b_ref):
        u = u_ref[...]
        w = w_ref[...]
        prev = jnp.where(row == 0, 0.0, pltpu.roll(u, 1, axis=0))
        nxt = jnp.where(row == T - 1, 0.0, pltpu.roll(u, T - 1, axis=0))
        return prev * w[0:1] + u * w[1:2] + nxt * w[2:3] + b_ref[...]

    x0_ref[...] = sconv(u0_ref, w0_ref, b0_ref)
    z = sconv(u2_ref, w2_ref, b2_ref) * sconv(u1_ref, w1_ref, b1_ref)
    z_ref[...] = z
    zb_ref[...] = z.astype(BF16)


def _hy_pre(uh, short_w, short_b):
    B, T, _ = uh.shape
    nb = HY_WIDTH // LANE
    u_specs = [pl.BlockSpec((None, T, LANE), lambda b, j, p=p: (b, 0, p * nb + j)) for p in range(3)]
    w_specs = [pl.BlockSpec((3, LANE), lambda b, j, p=p: (0, p * nb + j)) for p in range(3)]
    b_specs = [pl.BlockSpec((1, LANE), lambda b, j, p=p: (0, p * nb + j)) for p in range(3)]
    out_spec = pl.BlockSpec((None, T, LANE), lambda b, j: (b, 0, j))
    sb = short_b.reshape(1, HY_U)
    return pl.pallas_call(
        _hy_pre_kernel,
        grid=(B, nb),
        in_specs=u_specs + w_specs + b_specs,
        out_specs=[out_spec] * 3,
        out_shape=[jax.ShapeDtypeStruct((B, T, HY_WIDTH), F32),
                   jax.ShapeDtypeStruct((B, T, HY_WIDTH), F32),
                   jax.ShapeDtypeStruct((B, T, HY_WIDTH), BF16)],
        compiler_params=_cparams("parallel", "parallel"),
        name="hyena_short_conv",
    )(uh, uh, uh, short_w, short_w, short_w, sb, sb, sb)


def _filter_kernel(z_ref, w1_ref, b1_ref, w2_ref, b2_ref, w3_ref, b3_ref, w4f_ref, w4b_ref,
                   fr_ref, t_ref, dl_ref, fe_ref, fo_ref, h_ref):
    hp = lax.Precision.HIGHEST

    @pl.when(pl.program_id(0) == 0)
    def _():
        fr = fr_ref[...]
        h = jnp.sin(fr * (jnp.dot(z_ref[...], w1_ref[...], preferred_element_type=F32, precision=hp) + b1_ref[...]))
        h = jnp.sin(fr * (jnp.dot(h, w2_ref[...], preferred_element_type=F32, precision=hp) + b2_ref[...]))
        h_ref[...] = jnp.sin(fr * (jnp.dot(h, w3_ref[...], preferred_element_type=F32, precision=hp) + b3_ref[...]))

    h = h_ref[...]
    decay = jnp.exp(-t_ref[...] * jnp.abs(dl_ref[...]))
    fwd = jnp.dot(h, w4f_ref[...], preferred_element_type=F32, precision=hp) * decay
    bwd = jnp.dot(h, w4b_ref[...], preferred_element_type=F32, precision=hp) * decay
    row = lax.broadcasted_iota(jnp.int32, bwd.shape, 0)
    bwd = jnp.where(row == 0, 0.0, bwd)
    ss = jnp.sum(fwd * fwd + bwd * bwd, axis=0, keepdims=True)
    sc = lax.rsqrt(ss + EPS)
    fe_ref[...] = ((fwd + bwd) * sc).astype(fe_ref.dtype)
    fo_ref[...] = ((fwd - bwd) * sc).astype(fo_ref.dtype)


def _hyena_filter_eo(T, w1, b1, w2, b2, w3, b3, w4, freq):
    t = jnp.linspace(0.0, 1.0, T, dtype=F32)[:, None]
    w = 2.0 * math.pi * jnp.arange(T, dtype=F32)[:, None] / T
    f = jnp.linspace(1e-4, HY_BANDS - 1, HY_BANDS, dtype=F32)[None]
    z = jnp.concatenate([t, jnp.cos(f * w), -jnp.sin(f * w)], axis=-1)
    z = jnp.pad(z, ((0, 0), (0, LANE - HY_EMB)))
    w1p = jnp.pad(w1, ((0, LANE - HY_EMB), (0, 0)))
    deltas = jnp.linspace(HY_MIN_DECAY, HY_MAX_DECAY, HY_WIDTH, dtype=F32)[None]
    nb = HY_WIDTH // LANE
    c2 = lambda j: (0, 0)
    vec = lambda a: a.reshape(1, HY_FFN)
    return pl.pallas_call(
        _filter_kernel,
        grid=(nb,),
        in_specs=[pl.BlockSpec((T, LANE), c2),
                  pl.BlockSpec((LANE, HY_FFN), c2), pl.BlockSpec((1, HY_FFN), c2),
                  pl.BlockSpec((HY_FFN, HY_FFN), c2), pl.BlockSpec((1, HY_FFN), c2),
                  pl.BlockSpec((HY_FFN, HY_FFN), c2), pl.BlockSpec((1, HY_FFN), c2),
                  pl.BlockSpec((HY_FFN, LANE), lambda j: (0, j)),
                  pl.BlockSpec((HY_FFN, LANE), lambda j: (0, nb + j)),
                  pl.BlockSpec((1, HY_FFN), c2),
                  pl.BlockSpec((T, 1), c2),
                  pl.BlockSpec((1, LANE), lambda j: (0, j))],
        out_specs=[pl.BlockSpec((T, LANE), lambda j: (0, j))] * 2,
        out_shape=[jax.ShapeDtypeStruct((T, HY_WIDTH), BF16)] * 2,
        scratch_shapes=[pltpu.VMEM((T, HY_FFN), F32)],
        compiler_params=_cparams("arbitrary"),
        name="hyena_filter",
    )(z, w1p, vec(b1), w2, vec(b2), w3, vec(b3), w4, w4, vec(freq), t, deltas)


def _dft_tile_rows(T):
    return min(1024, 2 * T)


def _dft_tables(T):
    n = 2 * T
    tr = _dft_tile_rows(T)
    half = tr // 2
    f = DFT_FACTOR
    assert T % f == 0 and half % f == 0
    ang = 2.0 * math.pi / n

    def cs(k, s):
        a = ((k * s) % n).astype(F32) * ang
        return jnp.cos(a), jnp.sin(a)

    s = jnp.arange(T, dtype=jnp.int32)[None]
    cl, sl = cs(jnp.arange(f, dtype=jnp.int32)[:, None], s)
    ch, sh = cs(jnp.arange(T // f, dtype=jnp.int32)[:, None] * f, s)
    x = jnp.stack([cl, sl])[None, :, None]
    w = jnp.stack([-sl, cl])[None, :, None]
    ch5 = ch.reshape(n // tr, 1, half // f, 1, T)
    sh5 = sh.reshape(n // tr, 1, half // f, 1, T)
    a = (ch5 * x + sh5 * w).reshape(n, T).astype(BF16)
    nyq = jnp.where(jnp.arange(T) % 2 == 0, 1.0, -1.0).astype(BF16)
    a = a.at[half].set(nyq)

    r = np.arange(n)
    local = r % tr
    k = (r // tr) * half + local % half
    is_sin = local >= half
    is_nyq = is_sin & (k == 0)
    k = jnp.asarray(np.where(is_nyq, T, k), jnp.int32)[None]
    sel = jnp.asarray(is_sin & ~is_nyq)[None]
    c_lo, s_lo = cs(k, jnp.arange(f, dtype=jnp.int32)[:, None])
    c_hi, s_hi = cs(k, jnp.arange(T // f, dtype=jnp.int32)[:, None] * f)
    x_lo = jnp.where(sel, s_lo, c_lo)
    w_lo = jnp.where(sel, c_lo, -s_lo)
    at = (x_lo[None] * c_hi[:, None] + w_lo[None] * s_hi[:, None]).reshape(T, n).astype(BF16)
    return a, at


NYQ_ROWS = 16


def _dft_filter_kernel(a_ref, fe_ref, fo_ref, c_ref, s_ref, nyq_ref):
    half = a_ref.shape[0] // 2
    c_ref[...] = jnp.dot(a_ref[:half, :], fe_ref[...], preferred_element_type=F32)
    s_ref[...] = jnp.dot(a_ref[half:, :], fo_ref[...], preferred_element_type=F32)

    @pl.when(pl.program_id(0) == 0)
    def _():
        nyq_ref[...] = jnp.dot(a_ref[half:half + NYQ_ROWS, :], fe_ref[...], preferred_element_type=F32)


def _filter_spectrum(a, fe, fo):
    n, T = a.shape
    tr = _dft_tile_rows(T)
    half = tr // 2
    cosp, sinp, nyq = pl.pallas_call(
        _dft_filter_kernel,
        grid=(n // tr,),
        in_specs=[pl.BlockSpec((tr, T), lambda i: (i, 0)),
                  pl.BlockSpec((T, HY_WIDTH), lambda i: (0, 0)),
                  pl.BlockSpec((T, HY_WIDTH), lambda i: (0, 0))],
        out_specs=[pl.BlockSpec((half, HY_WIDTH), lambda i: (i, 0)),
                   pl.BlockSpec((half, HY_WIDTH), lambda i: (i, 0)),
                   pl.BlockSpec((NYQ_ROWS, HY_WIDTH), lambda i: (0, 0))],
        out_shape=[jax.ShapeDtypeStruct((T, HY_WIDTH), F32),
                   jax.ShapeDtypeStruct((T, HY_WIDTH), F32),
                   jax.ShapeDtypeStruct((NYQ_ROWS, HY_WIDTH), F32)],
        compiler_params=_cparams("arbitrary"),
        name="dft_filter",
    )(a, fe, fo)
    hr = cosp
    hi = -sinp
    hnyq = nyq[0:1]
    first = (jnp.arange(T) == 0)[:, None]
    ck = jnp.where(first, 1.0 / n, 2.0 / n)
    t1 = ck * hr
    t2 = jnp.where(first, 0.0, ck * hi)
    t4 = jnp.where(first, hnyq / n, ck * hr)
    return t1, t2, t4


def _dft_fwd_kernel(a_ref, z_ref, t1_ref, t2_ref, t4_ref, y_ref):
    half = a_ref.shape[0] // 2
    u = jnp.dot(a_ref[...], z_ref[...], preferred_element_type=F32)
    re, im = u[:half], u[half:]
    t2 = t2_ref[...]
    y_ref[:half, :] = (re * t1_ref[...] + im * t2).astype(y_ref.dtype)
    y_ref[half:, :] = (im * t4_ref[...] - re * t2).astype(y_ref.dtype)


def _dft_fwd(a, zb, t1, t2, t4):
    n, T = a.shape
    B = zb.shape[0]
    tr = _dft_tile_rows(T)
    tmap = lambda i, b: (i, 0)
    return pl.pallas_call(
        _dft_fwd_kernel,
        grid=(n // tr, B),
        in_specs=[pl.BlockSpec((tr, T), tmap),
                  pl.BlockSpec((None, T, HY_WIDTH), lambda i, b: (b, 0, 0)),
                  pl.BlockSpec((tr // 2, HY_WIDTH), tmap),
                  pl.BlockSpec((tr // 2, HY_WIDTH), tmap),
                  pl.BlockSpec((tr // 2, HY_WIDTH), tmap)],
        out_specs=pl.BlockSpec((None, tr, HY_WIDTH), lambda i, b: (b, i, 0)),
        out_shape=jax.ShapeDtypeStruct((B, n, HY_WIDTH), BF16),
        compiler_params=_cparams("parallel", "arbitrary"),
        name="dft_forward",
    )(a, zb, t1, t2, t4)


def _dft_inv_kernel(at_ref, y_ref, x0_ref, z_ref, skip_ref, o_ref):
    y = jnp.dot(at_ref[...], y_ref[...], preferred_element_type=F32)
    o_ref[...] = x0_ref[...] * (y + z_ref[...] * skip_ref[...])


def _dft_inv(at, y, x0, z, skip):
    T, n = at.shape
    B = y.shape[0]
    tt = min(512, T)
    rmap = lambda i, b: (b, i, 0)
    return pl.pallas_call(
        _dft_inv_kernel,
        grid=(T // tt, B),
        in_specs=[pl.BlockSpec((tt, n), lambda i, b: (i, 0)),
                  pl.BlockSpec((None, n, HY_WIDTH), lambda i, b: (b, 0, 0)),
                  pl.BlockSpec((None, tt, HY_WIDTH), rmap),
                  pl.BlockSpec((None, tt, HY_WIDTH), rmap),
                  pl.BlockSpec((1, HY_WIDTH), lambda i, b: (0, 0))],
        out_specs=pl.BlockSpec((None, tt, HY_WIDTH), rmap),
        out_shape=jax.ShapeDtypeStruct((B, T, HY_WIDTH), F32),
        compiler_params=_cparams("parallel", "arbitrary"),
        name="dft_inverse",
    )(at, y, x0, z, skip.reshape(1, HY_WIDTH))


def _hyena(uh, dft, spectrum, short_w, short_b, skip):
    a, at = dft
    x0, z, zb = _hy_pre(uh, short_w, short_b)
    y = _dft_fwd(a, zb, *spectrum)
    return _dft_inv(at, y, x0, z, skip)


def _outproj_kernel(ya_ref, ogq_ref, gg_ref, ohy_ref, gh_ref, x_ref, gate_ref, w_ref,
                    gqg_ref, hyg_ref, fg_ref, o_ref, *, final):
    yg = _rms_rows(ogq_ref[...], gqg_ref[...]) * gg_ref[...].astype(F32)
    yh = _rms_rows(ohy_ref[...], hyg_ref[...]) * gh_ref[...].astype(F32)
    a0, a1, a2 = DA_WIDTH, DA_WIDTH + GQ_WIDTH, D_MIX
    y = jnp.dot(ya_ref[...], w_ref[0:a0, :], preferred_element_type=F32)
    y = y + jnp.dot(yg.astype(BF16), w_ref[a0:a1, :], preferred_element_type=F32)
    y = y + jnp.dot(yh.astype(BF16), w_ref[a1:a2, :], preferred_element_type=F32)
    out = x_ref[...] + gate_ref[...] * y
    if final:
        out = _rms_rows(out, fg_ref[...])
    o_ref[...] = out


def _outproj(ya, ogq, gg, ohy, gh, xt, gate, w_bf, gq_out_g, hy_out_g, final_g, final):
    B, T, D = xt.shape
    tm = min(512, T)
    row = lambda b, i: (b, i, 0)
    c2 = lambda b, i: (0, 0)
    return pl.pallas_call(
        functools.partial(_outproj_kernel, final=final),
        grid=(B, T // tm),
        in_specs=[pl.BlockSpec((None, tm, DA_WIDTH), row),
                  pl.BlockSpec((None, tm, GQ_WIDTH), row),
                  pl.BlockSpec((None, tm, GQ_WIDTH), row),
                  pl.BlockSpec((None, tm, HY_WIDTH), row),
                  pl.BlockSpec((None, tm, HY_WIDTH), row),
                  pl.BlockSpec((None, tm, D), row),
                  pl.BlockSpec((None, 1, D), lambda b, i: (b, 0, 0)),
                  pl.BlockSpec((D_MIX, D), c2),
                  pl.BlockSpec((1, GQ_WIDTH), c2),
                  pl.BlockSpec((1, HY_WIDTH), c2),
                  pl.BlockSpec((1, D), c2)],
        out_specs=pl.BlockSpec((None, tm, D), row),
        out_shape=jax.ShapeDtypeStruct((B, T, D), F32),
        compiler_params=_cparams("parallel", "parallel"),
        name="outproj_final" if final else "outproj",
    )(ya, ogq, gg, ohy, gh, xt, gate, w_bf, gq_out_g.reshape(1, GQ_WIDTH),
      hy_out_g.reshape(1, HY_WIDTH), final_g.reshape(1, D))


def kernel(x, c, ctx, c_ctx, ada_w, ada_b, norm_g, w_in, w_out, da_lambda, da_subln_g, gq_q_g, gq_k_g, gq_out_g, hy_short_w, hy_short_b, hy_w1, hy_b1, hy_w2, hy_b2, hy_w3, hy_b3, hy_w4, hy_freq, hy_bias, hy_out_g, final_g):
    B, L, D = x.shape
    Lc = ctx.shape[1]
    depth = ada_w.shape[0]
    assert L % GRID_W == 0 and L % Lc == 0 and D_IN == w_in.shape[2]

    tabs = _rope_tables(L, DA_QK_DIM, 2) + _rope_tables(L, GQ_DIM, 1)
    dft_lat = _dft_tables(L)
    dft_ctx = _dft_tables(Lc)

    n_mod = B + 1
    pad = (-n_mod) % 8
    c_all = jnp.concatenate([c, c_ctx[None], jnp.zeros((pad, D), F32)], axis=0)
    mod = _modulation(c_all, ada_w, ada_b)

    w_in_bf = w_in.astype(BF16)
    w_out_bf = w_out.astype(BF16)

    xc = ctx
    for l in range(depth):
        update_ctx = l < depth - 1
        lam_init = 0.8 - 0.6 * math.exp(-0.3 * l)
        shift, scale, gate = (mod[l, :B, i * D:(i + 1) * D][:, None, :] for i in range(3))
        shift_c, scale_c, gate_c = (jnp.broadcast_to(mod[l, B:B + 1, i * D:(i + 1) * D][:, None, :], (B, 1, D))
                                    for i in range(3))

        qa, ka, va, ga, qg, kg, vg, gg, uh, gh = _inproj(
            x, scale, shift, norm_g[l], w_in_bf[l], gq_q_g[l], gq_k_g[l], tabs, L + Lc, 0, None)
        qa_c, ka, va, ga_c, qg_c, kg, vg, gg_c, uh_c, gh_c = _inproj(
            xc, scale_c, shift_c, norm_g[l], w_in_bf[l], gq_q_g[l], gq_k_g[l], None, L + Lc, L, (ka, va, kg, vg))

        ya = _da_attention(da_lambda[l], qa, ka, va, ga, da_subln_g[l], lam_init, L + Lc, 0)
        ogq = _gq_attention(qg, kg, vg, L + Lc, 0)

        filt = (hy_w1[l], hy_b1[l], hy_w2[l], hy_b2[l], hy_w3[l], hy_b3[l], hy_w4[l], hy_freq[l])
        spec = _filter_spectrum(dft_lat[0], *_hyena_filter_eo(L, *filt))
        ohy = _hyena(uh, dft_lat, spec, hy_short_w[l], hy_short_b[l], hy_bias[l])

        if update_ctx:
            ya_c = _da_attention(da_lambda[l], qa_c, ka, va, ga_c, da_subln_g[l], lam_init, Lc, L // Lc)
            ogq_c = _gq_attention(qg_c, kg, vg, Lc, L // Lc)
            spec_c = _filter_spectrum(dft_ctx[0], *_hyena_filter_eo(Lc, *filt))
            ohy_c = _hyena(uh_c, dft_ctx, spec_c, hy_short_w[l], hy_short_b[l], hy_bias[l])
            xc = _outproj(ya_c, ogq_c, gg_c, ohy_c, gh_c, xc, gate_c, w_out_bf[l],
                          gq_out_g[l], hy_out_g[l], final_g, False)

        x = _outproj(ya, ogq, gg, ohy, gh, x, gate, w_out_bf[l],
                     gq_out_g[l], hy_out_g[l], final_g, l == depth - 1)
    return x
```

```python
import functools
import math

import jax
import jax.numpy as jnp
import numpy as np
from jax import lax
from jax.experimental import pallas as pl
from jax.experimental.pallas import tpu as pltpu

F32 = jnp.float32
BF16 = jnp.bfloat16

GRID_W = 64
EPS = 1e-6
ROPE_THETA = 10000.0
DA_HEADS = 4
DA_QK_DIM = 64
DA_V_DIM = 2 * DA_QK_DIM
DA_WIDTH = DA_HEADS * DA_V_DIM
GQ_HEADS = 8
GQ_KV_HEADS = 2
GQ_GROUP = GQ_HEADS // GQ_KV_HEADS
GQ_DIM = 128
GQ_WIDTH = GQ_HEADS * GQ_DIM
HY_WIDTH = 512
HY_EMB = 33
HY_BANDS = (HY_EMB - 1) // 2
HY_FFN = 64
HY_MAX_DECAY = math.log(1e-2) / 0.3
HY_MIN_DECAY = math.log(1e-2) / 1.5
HY_U = 3 * HY_WIDTH
D_MIX = DA_WIDTH + GQ_WIDTH + HY_WIDTH
COL_SIZES = (DA_WIDTH, DA_WIDTH, DA_WIDTH, DA_WIDTH,
             GQ_WIDTH, GQ_KV_HEADS * GQ_DIM, GQ_KV_HEADS * GQ_DIM, GQ_WIDTH,
             HY_U, HY_WIDTH)
COL_OFFS = tuple(int(v) for v in np.cumsum((0,) + COL_SIZES))
D_IN = COL_OFFS[-1]

LANE = 128
ATTN_CHAIN_ROWS = 512
ATTN_CHAINS = 8
ATTN_LOOKAHEAD = 1
DFT_FACTOR = 64
LOG2E = math.log2(math.e)
ONES_ROWS = 16
KEY_GROUP_ROWS = 256
VMEM_LIMIT = 56 * 1024 * 1024


def _cparams(*sem):
    return pltpu.CompilerParams(dimension_semantics=sem, vmem_limit_bytes=VMEM_LIMIT)


def _silu(g):
    return g / (1.0 + jnp.exp(-g))


def _rms_rows(x, g):
    return x * lax.rsqrt(jnp.mean(x * x, axis=-1, keepdims=True) + EPS) * g


def _mod_kernel(c_ref, w_ref, b_ref, o_ref):
    c = c_ref[...]
    o_ref[...] = jnp.dot(_silu(c), w_ref[...], preferred_element_type=F32,
                         precision=lax.Precision.HIGHEST) + b_ref[...]


def _modulation(c_all, ada_w, ada_b):
    depth, d, n3 = ada_w.shape
    r = c_all.shape[0]
    tn = 512
    return pl.pallas_call(
        _mod_kernel,
        grid=(depth, n3 // tn),
        in_specs=[pl.BlockSpec((r, d), lambda l, j: (0, 0)),
                  pl.BlockSpec((None, d, tn), lambda l, j: (l, 0, j)),
                  pl.BlockSpec((None, 1, tn), lambda l, j: (l, 0, j))],
        out_specs=pl.BlockSpec((None, r, tn), lambda l, j: (l, 0, j)),
        out_shape=jax.ShapeDtypeStruct((depth, r, n3), F32),
        compiler_params=_cparams("parallel", "parallel"),
        name="modulation",
    )(c_all, ada_w, ada_b.reshape(depth, 1, n3))


def _rope_tables(L, head_dim, reps):
    rows_n = L // GRID_W
    row = jnp.repeat(jnp.arange(rows_n), GRID_W).astype(F32)
    col = jnp.tile(jnp.arange(GRID_W), rows_n).astype(F32)
    axis_dim = head_dim // 2
    inv = ROPE_THETA ** (-jnp.arange(0, axis_dim, 2, dtype=F32) / axis_dim)
    ar, ac = row[:, None] * inv[None], col[:, None] * inv[None]
    z = jnp.zeros_like(ar)
    cos = jnp.concatenate([jnp.cos(ar), jnp.cos(ar), jnp.cos(ac), jnp.cos(ac)], axis=-1)
    s_lo = jnp.concatenate([-jnp.sin(ar), z, -jnp.sin(ac), z], axis=-1)
    s_hi = jnp.concatenate([z, jnp.sin(ar), z, jnp.sin(ac)], axis=-1)
    return tuple(jnp.tile(t, (1, reps)) for t in (cos, s_lo, s_hi))


def _rope_block(x, cos, s_lo, s_hi, half):
    fwd = pltpu.roll(x, LANE - half, axis=1)
    bwd = pltpu.roll(x, half, axis=1)
    return x * cos + fwd * s_lo + bwd * s_hi


def _inproj_kernel(*refs, rope, n_alias):
    x_ref, sc_ref, sh_ref, ng_ref, w_ref, qg_g_ref, kg_g_ref = refs[:7]
    n_in = 7 + (6 if rope else 0)
    tabs = refs[7:13] if rope else None
    qa_o, ka_o, va_o, ga_o, qg_o, kg_o, vg_o, gg_o, uh_o, gh_o = refs[n_in + n_alias:]

    x = x_ref[...]
    h = _rms_rows(x, ng_ref[...])
    h = h * (1.0 + sc_ref[...]) + sh_ref[...]
    hb = h.astype(BF16)

    def proj(group):
        off = COL_OFFS[group]
        return jnp.dot(hb, w_ref[:, off:off + COL_SIZES[group]], preferred_element_type=F32)

    def blocks(v):
        return [(slice(b * LANE, (b + 1) * LANE), v[:, b * LANE:(b + 1) * LANE])
                for b in range(v.shape[1] // LANE)]

    def rope_da(v):
        if not rope:
            return v
        return _rope_block(v, tabs[0][...], tabs[1][...], tabs[2][...], DA_QK_DIM // 4)

    def rope_gq(v):
        if not rope:
            return v
        return _rope_block(v, tabs[3][...], tabs[4][...], tabs[5][...], GQ_DIM // 4)

    for sl, v in blocks(proj(0)):
        qa_o[:, sl] = (rope_da(v) * (LOG2E * DA_QK_DIM ** -0.5)).astype(qa_o.dtype)
    for sl, v in blocks(proj(1)):
        ka_o[:, sl] = rope_da(v).astype(ka_o.dtype)
    va_o[...] = proj(2).T.astype(va_o.dtype)
    ga_o[...] = _silu(proj(3)).astype(ga_o.dtype)
    for sl, v in blocks(proj(4)):
        qg_o[:, sl] = (rope_gq(_rms_rows(v, qg_g_ref[...])) * (LOG2E * GQ_DIM ** -0.5)).astype(qg_o.dtype)
    for sl, v in blocks(proj(5)):
        kg_o[:, sl] = rope_gq(_rms_rows(v, kg_g_ref[...])).astype(kg_o.dtype)
    vg_o[...] = proj(6).T.astype(vg_o.dtype)
    gg_o[...] = _silu(proj(7)).astype(gg_o.dtype)
    uh_o[...] = proj(8)
    gh_o[...] = _silu(proj(9)).astype(gh_o.dtype)


def _inproj(xt, scale, shift, norm_g, w_bf, q_g, k_g, tabs, kv_rows, kv_row0, kv_bufs):
    B, T, D = xt.shape
    tm = min(512, T)
    assert kv_row0 % tm == 0
    rope = tabs is not None
    blk0 = kv_row0 // tm
    row = lambda b, i: (b, i, 0)
    krow = lambda b, i: (b, i + blk0, 0)
    vcol = lambda b, i: (b, 0, i + blk0)
    const2 = lambda b, i: (0, 0)
    in_specs = [pl.BlockSpec((None, tm, D), row),
                pl.BlockSpec((None, 1, D), lambda b, i: (b, 0, 0)),
                pl.BlockSpec((None, 1, D), lambda b, i: (b, 0, 0)),
                pl.BlockSpec((1, D), const2),
                pl.BlockSpec((D, D_IN), const2, pipeline_mode=pl.Buffered(1)),
                pl.BlockSpec((1, GQ_DIM), const2),
                pl.BlockSpec((1, GQ_DIM), const2)]
    args = [xt, scale, shift, norm_g.reshape(1, D), w_bf, q_g.reshape(1, GQ_DIM), k_g.reshape(1, GQ_DIM)]
    if rope:
        in_specs += [pl.BlockSpec((tm, LANE), lambda b, i: (i, 0))] * 6
        args += list(tabs)
    aliases = {}
    if kv_bufs is not None:
        kv_out_index = (1, 2, 5, 6)
        aliases = {len(args) + n: o for n, o in enumerate(kv_out_index)}
        in_specs += [pl.BlockSpec(memory_space=pl.ANY)] * len(kv_bufs)
        args += list(kv_bufs)
    kvw = GQ_KV_HEADS * GQ_DIM
    tok = lambda w, dt: (pl.BlockSpec((None, tm, w), row), jax.ShapeDtypeStruct((B, T, w), dt))
    keys = lambda w: (pl.BlockSpec((None, tm, w), krow), jax.ShapeDtypeStruct((B, kv_rows, w), BF16))
    vals = lambda w: (pl.BlockSpec((None, w, tm), vcol), jax.ShapeDtypeStruct((B, w, kv_rows), BF16))
    outs = [tok(DA_WIDTH, BF16), keys(DA_WIDTH), vals(DA_WIDTH), tok(DA_WIDTH, BF16),
            tok(GQ_WIDTH, BF16), keys(kvw), vals(kvw), tok(GQ_WIDTH, BF16),
            tok(HY_U, F32), tok(HY_WIDTH, BF16)]
    return pl.pallas_call(
        functools.partial(_inproj_kernel, rope=rope, n_alias=len(aliases)),
        grid=(B, T // tm),
        in_specs=in_specs,
        out_specs=[o[0] for o in outs],
        out_shape=[o[1] for o in outs],
        input_output_aliases=aliases,
        compiler_params=_cparams("parallel", "parallel"),
        name="inproj_rope" if rope else "inproj_ctx",
    )(*args)


def _reduce_keys(a, op):
    tk, m = a.shape
    g = KEY_GROUP_ROWS if tk % KEY_GROUP_ROWS == 0 else tk
    part = op(a.reshape(tk // g, g, m), axis=0)
    return op(part, axis=0, keepdims=True)


def _scores_t(qs, k):
    return lax.dot_general(k, qs, (((1,), (1,)), ((), ())), preferred_element_type=F32)


def _softmax_pv_t(st, vt):
    m = _reduce_keys(st, jnp.max)
    p = jnp.exp2(st - m).astype(BF16)
    ot = jnp.dot(vt, p, preferred_element_type=F32)
    return (ot[:LANE] / ot[LANE:LANE + 1]).T


def _attend_chains(queries, k_ref, vt_ref):
    n = len(queries)
    vt = jnp.concatenate([vt_ref[...], jnp.ones((ONES_ROWS, vt_ref.shape[1]), BF16)], axis=0)
    scores = [_scores_t(queries[j](), k_ref[...]) for j in range(min(ATTN_LOOKAHEAD, n))]
    outs = []
    for i in range(n):
        if i + ATTN_LOOKAHEAD < n:
            scores.append(_scores_t(queries[i + ATTN_LOOKAHEAD](), k_ref[...]))
        outs.append(_softmax_pv_t(scores[i], vt))
        scores[i] = None
    return outs


def _da_kernel(lam_ref, q_ref, k_ref, v_ref, gate_ref, sg_ref, o_ref, *, lam_init):
    tq = q_ref.shape[0]
    lm = lam_ref[...]
    lam = (jnp.exp(jnp.sum(lm[0:1] * lm[1:2], axis=-1, keepdims=True))
           - jnp.exp(jnp.sum(lm[2:3] * lm[3:4], axis=-1, keepdims=True)) + lam_init)
    ch = min(ATTN_CHAIN_ROWS, tq)

    def component(r, c):
        def load():
            q = q_ref[r * ch:(r + 1) * ch, :]
            lane = lax.broadcasted_iota(jnp.int32, q.shape, 1)
            keep = (lane < DA_QK_DIM) if c == 0 else (lane >= DA_QK_DIM)
            return jnp.where(keep, q, jnp.zeros_like(q))
        return load

    n = tq // ch
    outs = _attend_chains([component(r, c) for r in range(n) for c in range(2)], k_ref, v_ref)
    for r in range(n):
        rows = slice(r * ch, (r + 1) * ch)
        o = _rms_rows(outs[2 * r] - lam * outs[2 * r + 1], sg_ref[...]) * (1.0 - lam_init)
        o_ref[rows, :] = (o * gate_ref[rows, :].astype(F32)).astype(o_ref.dtype)


def _da_attention(lam_p, q, k, v, gate, subln_g, lam_init, Tk, kblk):
    B, Tq, _ = q.shape
    tq = min(ATTN_CHAINS * ATTN_CHAIN_ROWS // 2, Tq)
    qmap = lambda b, h, i: (b, i, h)
    kmap = lambda b, h, i: (b, kblk, h)
    vmap = lambda b, h, i: (b, h, kblk)
    return pl.pallas_call(
        functools.partial(_da_kernel, lam_init=lam_init),
        grid=(B, DA_HEADS, Tq // tq),
        in_specs=[pl.BlockSpec((4, DA_QK_DIM), lambda b, h, i: (0, 0)),
                  pl.BlockSpec((None, tq, LANE), qmap),
                  pl.BlockSpec((None, Tk, LANE), kmap),
                  pl.BlockSpec((None, LANE, Tk), vmap),
                  pl.BlockSpec((None, tq, LANE), qmap),
                  pl.BlockSpec((1, LANE), lambda b, h, i: (0, 0))],
        out_specs=pl.BlockSpec((None, tq, LANE), qmap),
        out_shape=jax.ShapeDtypeStruct((B, Tq, DA_WIDTH), BF16),
        compiler_params=_cparams("parallel", "parallel", "arbitrary"),
        name="diff_attention",
    )(lam_p, q, k, v, gate, subln_g.reshape(1, DA_V_DIM))


def _gq_kernel(q_ref, k_ref, v_ref, o_ref):
    tq = q_ref.shape[0]
    ch = min(ATTN_CHAIN_ROWS, tq)
    where = [(slice(c * ch, (c + 1) * ch), slice(r * LANE, (r + 1) * LANE))
             for c in range(tq // ch) for r in range(GQ_GROUP)]
    outs = _attend_chains([functools.partial(lambda w: q_ref[w[0], w[1]], w) for w in where], k_ref, v_ref)
    for (rows, sl), o in zip(where, outs):
        o_ref[rows, sl] = o.astype(o_ref.dtype)


def _gq_attention(q, k, v, Tk, kblk):
    B, Tq, _ = q.shape
    tq = min(ATTN_CHAINS * ATTN_CHAIN_ROWS // GQ_GROUP, Tq)
    gw = GQ_GROUP * GQ_DIM
    qmap = lambda b, g, i: (b, i, g)
    kmap = lambda b, g, i: (b, kblk, g)
    vmap = lambda b, g, i: (b, g, kblk)
    return pl.pallas_call(
        _gq_kernel,
        grid=(B, GQ_KV_HEADS, Tq // tq),
        in_specs=[pl.BlockSpec((None, tq, gw), qmap),
                  pl.BlockSpec((None, Tk, LANE), kmap),
                  pl.BlockSpec((None, LANE, Tk), vmap)],
        out_specs=pl.BlockSpec((None, tq, gw), qmap),
        out_shape=jax.ShapeDtypeStruct((B, Tq, GQ_WIDTH), F32),
        compiler_params=_cparams("parallel", "parallel", "arbitrary"),
        name="gqa_attention",
    )(q, k, v)


def _hy_pre_kernel(u0_ref, u1_ref, u2_ref, w0_ref, w1_ref, w2_ref, b0_ref, b1_ref, b2_ref,
                   x0_ref, z_ref, zb_ref):
    T = u0_ref.shape[0]
    row = lax.broadcasted_iota(jnp.int32, u0_ref.shape, 0)

    def sconv(u_ref, w_ref, b_ref):
        u = u_ref[...]
        w = w_ref[...]
        prev = jnp.where(row == 0, 0.0, pltpu.roll(u, 1, axis=0))
        nxt = jnp.where(row == T - 1, 0.0, pltpu.roll(u, T - 1, axis=0))
        return prev * w[0:1] + u * w[1:2] + nxt * w[2:3] + b_ref[...]

    x0_ref[...] = sconv(u0_ref, w0_ref, b0_ref)
    z = sconv(u2_ref, w2_ref, b2_ref) * sconv(u1_ref, w1_ref, b1_ref)
    z_ref[...] = z
    zb_ref[...] = z.astype(BF16)


def _hy_pre(uh, short_w, short_b):
    B, T, _ = uh.shape
    nb = HY_WIDTH // LANE
    u_specs = [pl.BlockSpec((None, T, LANE), lambda b, j, p=p: (b, 0, p * nb + j)) for p in range(3)]
    w_specs = [pl.BlockSpec((3, LANE), lambda b, j, p=p: (0, p * nb + j)) for p in range(3)]
    b_specs = [pl.BlockSpec((1, LANE), lambda b, j, p=p: (0, p * nb + j)) for p in range(3)]
    out_spec = pl.BlockSpec((None, T, LANE), lambda b, j: (b, 0, j))
    sb = short_b.reshape(1, HY_U)
    return pl.pallas_call(
        _hy_pre_kernel,
        grid=(B, nb),
        in_specs=u_specs + w_specs + b_specs,
        out_specs=[out_spec] * 3,
        out_shape=[jax.ShapeDtypeStruct((B, T, HY_WIDTH), F32),
                   jax.ShapeDtypeStruct((B, T, HY_WIDTH), F32),
                   jax.ShapeDtypeStruct((B, T, HY_WIDTH), BF16)],
        compiler_params=_cparams("parallel", "parallel"),
        name="hyena_short_conv",
    )(uh, uh, uh, short_w, short_w, short_w, sb, sb, sb)


def _filter_kernel(z_ref, w1_ref, b1_ref, w2_ref, b2_ref, w3_ref, b3_ref, w4f_ref, w4b_ref,
                   fr_ref, t_ref, dl_ref, fe_ref, fo_ref, h_ref):
    hp = lax.Precision.HIGHEST

    @pl.when(pl.program_id(0) == 0)
    def _():
        fr = fr_ref[...]
        h = jnp.sin(fr * (jnp.dot(z_ref[...], w1_ref[...], preferred_element_type=F32, precision=hp) + b1_ref[...]))
        h = jnp.sin(fr * (jnp.dot(h, w2_ref[...], preferred_element_type=F32, precision=hp) + b2_ref[...]))
        h_ref[...] = jnp.sin(fr * (jnp.dot(h, w3_ref[...], preferred_element_type=F32, precision=hp) + b3_ref[...]))

    h = h_ref[...]
    decay = jnp.exp(-t_ref[...] * jnp.abs(dl_ref[...]))
    fwd = jnp.dot(h, w4f_ref[...], preferred_element_type=F32, precision=hp) * decay
    bwd = jnp.dot(h, w4b_ref[...], preferred_element_type=F32, precision=hp) * decay
    row = lax.broadcasted_iota(jnp.int32, bwd.shape, 0)
    bwd = jnp.where(row == 0, 0.0, bwd)
    ss = jnp.sum(fwd * fwd + bwd * bwd, axis=0, keepdims=True)
    sc = lax.rsqrt(ss + EPS)
    fe_ref[...] = ((fwd + bwd) * sc).astype(fe_ref.dtype)
    fo_ref[...] = ((fwd - bwd) * sc).astype(fo_ref.dtype)


def _hyena_filter_eo(T, w1, b1, w2, b2, w3, b3, w4, freq):
    t = jnp.linspace(0.0, 1.0, T, dtype=F32)[:, None]
    w = 2.0 * math.pi * jnp.arange(T, dtype=F32)[:, None] / T
    f = jnp.linspace(1e-4, HY_BANDS - 1, HY_BANDS, dtype=F32)[None]
    z = jnp.concatenate([t, jnp.cos(f * w), -jnp.sin(f * w)], axis=-1)
    z = jnp.pad(z, ((0, 0), (0, LANE - HY_EMB)))
    w1p = jnp.pad(w1, ((0, LANE - HY_EMB), (0, 0)))
    deltas = jnp.linspace(HY_MIN_DECAY, HY_MAX_DECAY, HY_WIDTH, dtype=F32)[None]
    nb = HY_WIDTH // LANE
    c2 = lambda j: (0, 0)
    vec = lambda a: a.reshape(1, HY_FFN)
    return pl.pallas_call(
        _filter_kernel,
        grid=(nb,),
        in_specs=[pl.BlockSpec((T, LANE), c2),
                  pl.BlockSpec((LANE, HY_FFN), c2), pl.BlockSpec((1, HY_FFN), c2),
                  pl.BlockSpec((HY_FFN, HY_FFN), c2), pl.BlockSpec((1, HY_FFN), c2),
                  pl.BlockSpec((HY_FFN, HY_FFN), c2), pl.BlockSpec((1, HY_FFN), c2),
                  pl.BlockSpec((HY_FFN, LANE), lambda j: (0, j)),
                  pl.BlockSpec((HY_FFN, LANE), lambda j: (0, nb + j)),
                  pl.BlockSpec((1, HY_FFN), c2),
                  pl.BlockSpec((T, 1), c2),
                  pl.BlockSpec((1, LANE), lambda j: (0, j))],
        out_specs=[pl.BlockSpec((T, LANE), lambda j: (0, j))] * 2,
        out_shape=[jax.ShapeDtypeStruct((T, HY_WIDTH), BF16)] * 2,
        scratch_shapes=[pltpu.VMEM((T, HY_FFN), F32)],
        compiler_params=_cparams("arbitrary"),
        name="hyena_filter",
    )(z, w1p, vec(b1), w2, vec(b2), w3, vec(b3), w4, w4, vec(freq), t, deltas)


def _dft_tile_rows(T):
    return min(1024, 2 * T)


def _dft_tables(T):
    n = 2 * T
    tr = _dft_tile_rows(T)
    half = tr // 2
    f = DFT_FACTOR
    assert T % f == 0 and half % f == 0
    ang = 2.0 * math.pi / n

    def cs(k, s):
        a = ((k * s) % n).astype(F32) * ang
        return jnp.cos(a), jnp.sin(a)

    s = jnp.arange(T, dtype=jnp.int32)[None]
    cl, sl = cs(jnp.arange(f, dtype=jnp.int32)[:, None], s)
    ch, sh = cs(jnp.arange(T // f, dtype=jnp.int32)[:, None] * f, s)
    x = jnp.stack([cl, sl])[None, :, None]
    w = jnp.stack([-sl, cl])[None, :, None]
    ch5 = ch.reshape(n // tr, 1, half // f, 1, T)
    sh5 = sh.reshape(n // tr, 1, half // f, 1, T)
    a = (ch5 * x + sh5 * w).reshape(n, T).astype(BF16)
    nyq = jnp.where(jnp.arange(T) % 2 == 0, 1.0, -1.0).astype(BF16)
    a = a.at[half].set(nyq)

    r = np.arange(n)
    local = r % tr
    k = (r // tr) * half + local % half
    is_sin = local >= half
    is_nyq = is_sin & (k == 0)
    k = jnp.asarray(np.where(is_nyq, T, k), jnp.int32)[None]
    sel = jnp.asarray(is_sin & ~is_nyq)[None]
    c_lo, s_lo = cs(k, jnp.arange(f, dtype=jnp.int32)[:, None])
    c_hi, s_hi = cs(k, jnp.arange(T // f, dtype=jnp.int32)[:, None] * f)
    x_lo = jnp.where(sel, s_lo, c_lo)
    w_lo = jnp.where(sel, c_lo, -s_lo)
    at = (x_lo[None] * c_hi[:, None] + w_lo[None] * s_hi[:, None]).reshape(T, n).astype(BF16)
    return a, at


NYQ_ROWS = 16


def _dft_filter_kernel(a_ref, fe_ref, fo_ref, c_ref, s_ref, nyq_ref):
    half = a_ref.shape[0] // 2
    c_ref[...] = jnp.dot(a_ref[:half, :], fe_ref[...], preferred_element_type=F32)
    s_ref[...] = jnp.dot(a_ref[half:, :], fo_ref[...], preferred_element_type=F32)

    @pl.when(pl.program_id(0) == 0)
    def _():
        nyq_ref[...] = jnp.dot(a_ref[half:half + NYQ_ROWS, :], fe_ref[...], preferred_element_type=F32)


def _filter_spectrum(a, fe, fo):
    n, T = a.shape
    tr = _dft_tile_rows(T)
    half = tr // 2
    cosp, sinp, nyq = pl.pallas_call(
        _dft_filter_kernel,
        grid=(n // tr,),
        in_specs=[pl.BlockSpec((tr, T), lambda i: (i, 0)),
                  pl.BlockSpec((T, HY_WIDTH), lambda i: (0, 0)),
                  pl.BlockSpec((T, HY_WIDTH), lambda i: (0, 0))],
        out_specs=[pl.BlockSpec((half, HY_WIDTH), lambda i: (i, 0)),
                   pl.BlockSpec((half, HY_WIDTH), lambda i: (i, 0)),
                   pl.BlockSpec((NYQ_ROWS, HY_WIDTH), lambda i: (0, 0))],
        out_shape=[jax.ShapeDtypeStruct((T, HY_WIDTH), F32),
                   jax.ShapeDtypeStruct((T, HY_WIDTH), F32),
                   jax.ShapeDtypeStruct((NYQ_ROWS, HY_WIDTH), F32)],
        compiler_params=_cparams("arbitrary"),
        name="dft_filter",
    )(a, fe, fo)
    hr = cosp
    hi = -sinp
    hnyq = nyq[0:1]
    first = (jnp.arange(T) == 0)[:, None]
    ck = jnp.where(first, 1.0 / n, 2.0 / n)
    t1 = ck * hr
    t2 = jnp.where(first, 0.0, ck * hi)
    t4 = jnp.where(first, hnyq / n, ck * hr)
    return t1, t2, t4


def _dft_fwd_kernel(a_ref, z_ref, t1_ref, t2_ref, t4_ref, y_ref):
    half = a_ref.shape[0] // 2
    u = jnp.dot(a_ref[...], z_ref[...], preferred_element_type=F32)
    re, im = u[:half], u[half:]
    t2 = t2_ref[...]
    y_ref[:half, :] = (re * t1_ref[...] + im * t2).astype(y_ref.dtype)
    y_ref[half:, :] = (im * t4_ref[...] - re * t2).astype(y_ref.dtype)


def _dft_fwd(a, zb, t1, t2, t4):
    n, T = a.shape
    B = zb.shape[0]
    tr = _dft_tile_rows(T)
    tmap = lambda i, b: (i, 0)
    return pl.pallas_call(
        _dft_fwd_kernel,
        grid=(n // tr, B),
        in_specs=[pl.BlockSpec((tr, T), tmap),
                  pl.BlockSpec((None, T, HY_WIDTH), lambda i, b: (b, 0, 0)),
                  pl.BlockSpec((tr // 2, HY_WIDTH), tmap),
                  pl.BlockSpec((tr // 2, HY_WIDTH), tmap),
                  pl.BlockSpec((tr // 2, HY_WIDTH), tmap)],
        out_specs=pl.BlockSpec((None, tr, HY_WIDTH), lambda i, b: (b, i, 0)),
        out_shape=jax.ShapeDtypeStruct((B, n, HY_WIDTH), BF16),
        compiler_params=_cparams("parallel", "arbitrary"),
        name="dft_forward",
    )(a, zb, t1, t2, t4)


def _dft_inv_kernel(at_ref, y_ref, x0_ref, z_ref, skip_ref, o_ref):
    y = jnp.dot(at_ref[...], y_ref[...], preferred_element_type=F32)
    o_ref[...] = x0_ref[...] * (y + z_ref[...] * skip_ref[...])


def _dft_inv(at, y, x0, z, skip):
    T, n = at.shape
    B = y.shape[0]
    tt = min(512, T)
    rmap = lambda i, b: (b, i, 0)
    return pl.pallas_call(
        _dft_inv_kernel,
        grid=(T // tt, B),
        in_specs=[pl.BlockSpec((tt, n), lambda i, b: (i, 0)),
                  pl.BlockSpec((None, n, HY_WIDTH), lambda i, b: (b, 0, 0)),
                  pl.BlockSpec((None, tt, HY_WIDTH), rmap),
                  pl.BlockSpec((None, tt, HY_WIDTH), rmap),
                  pl.BlockSpec((1, HY_WIDTH), lambda i, b: (0, 0))],
        out_specs=pl.BlockSpec((None, tt, HY_WIDTH), rmap),
        out_shape=jax.ShapeDtypeStruct((B, T, HY_WIDTH), F32),
        compiler_params=_cparams("parallel", "arbitrary"),
        name="dft_inverse",
    )(at, y, x0, z, skip.reshape(1, HY_WIDTH))


def _hyena(uh, dft, spectrum, short_w, short_b, skip):
    a, at = dft
    x0, z, zb = _hy_pre(uh, short_w, short_b)
    y = _dft_fwd(a, zb, *spectrum)
    return _dft_inv(at, y, x0, z, skip)


def _outproj_kernel(ya_ref, ogq_ref, gg_ref, ohy_ref, gh_ref, x_ref, gate_ref, w_ref,
                    gqg_ref, hyg_ref, fg_ref, o_ref, *, final):
    yg = _rms_rows(ogq_ref[...], gqg_ref[...]) * gg_ref[...].astype(F32)
    yh = _rms_rows(ohy_ref[...], hyg_ref[...]) * gh_ref[...].astype(F32)
    a0, a1, a2 = DA_WIDTH, DA_WIDTH + GQ_WIDTH, D_MIX
    y = jnp.dot(ya_ref[...], w_ref[0:a0, :], preferred_element_type=F32)
    y = y + jnp.dot(yg.astype(BF16), w_ref[a0:a1, :], preferred_element_type=F32)
    y = y + jnp.dot(yh.astype(BF16), w_ref[a1:a2, :], preferred_element_type=F32)
    out = x_ref[...] + gate_ref[...] * y
    if final:
        out = _rms_rows(out, fg_ref[...])
    o_ref[...] = out


def _outproj(ya, ogq, gg, ohy, gh, xt, gate, w_bf, gq_out_g, hy_out_g, final_g, final):
    B, T, D = xt.shape
    tm = min(512, T)
    row = lambda b, i: (b, i, 0)
    c2 = lambda b, i: (0, 0)
    return pl.pallas_call(
        functools.partial(_outproj_kernel, final=final),
        grid=(B, T // tm),
        in_specs=[pl.BlockSpec((None, tm, DA_WIDTH), row),
                  pl.BlockSpec((None, tm, GQ_WIDTH), row),
                  pl.BlockSpec((None, tm, GQ_WIDTH), row),
                  pl.BlockSpec((None, tm, HY_WIDTH), row),
                  pl.BlockSpec((None, tm, HY_WIDTH), row),
                  pl.BlockSpec((None, tm, D), row),
                  pl.BlockSpec((None, 1, D), lambda b, i: (b, 0, 0)),
                  pl.BlockSpec((D_MIX, D), c2),
                  pl.BlockSpec((1, GQ_WIDTH), c2),
                  pl.BlockSpec((1, HY_WIDTH), c2),
                  pl.BlockSpec((1, D), c2)],
        out_specs=pl.BlockSpec((None, tm, D), row),
        out_shape=jax.ShapeDtypeStruct((B, T, D), F32),
        compiler_params=_cparams("parallel", "parallel"),
        name="outproj_final" if final else "outproj",
    )(ya, ogq, gg, ohy, gh, xt, gate, w_bf, gq_out_g.reshape(1, GQ_WIDTH),
      hy_out_g.reshape(1, HY_WIDTH), final_g.reshape(1, D))


def kernel(x, c, ctx, c_ctx, ada_w, ada_b, norm_g, w_in, w_out, da_lambda, da_subln_g, gq_q_g, gq_k_g, gq_out_g, hy_short_w, hy_short_b, hy_w1, hy_b1, hy_w2, hy_b2, hy_w3, hy_b3, hy_w4, hy_freq, hy_bias, hy_out_g, final_g):
    B, L, D = x.shape
    Lc = ctx.shape[1]
    depth = ada_w.shape[0]
    assert L % GRID_W == 0 and L % Lc == 0 and D_IN == w_in.shape[2]

    tabs = _rope_tables(L, DA_QK_DIM, 2) + _rope_tables(L, GQ_DIM, 1)
    dft_lat = _dft_tables(L)
    dft_ctx = _dft_tables(Lc)

    n_mod = B + 1
    pad = (-n_mod) % 8
    c_all = jnp.concatenate([c, c_ctx[None], jnp.zeros((pad, D), F32)], axis=0)
    mod = _modulation(c_all, ada_w, ada_b)

    w_in_bf = w_in.astype(BF16)
    w_out_bf = w_out.astype(BF16)

    xc = ctx
    for l in range(depth):
        update_ctx = l < depth - 1
        lam_init = 0.8 - 0.6 * math.exp(-0.3 * l)
        shift, scale, gate = (mod[l, :B, i * D:(i + 1) * D][:, None, :] for i in range(3))
        shift_c, scale_c, gate_c = (jnp.broadcast_to(mod[l, B:B + 1, i * D:(i + 1) * D][:, None, :], (B, 1, D))
                                    for i in range(3))

        qa, ka, va, ga, qg, kg, vg, gg, uh, gh = _inproj(
            x, scale, shift, norm_g[l], w_in_bf[l], gq_q_g[l], gq_k_g[l], tabs, L + Lc, 0, None)
        qa_c, ka, va, ga_c, qg_c, kg, vg, gg_c, uh_c, gh_c = _inproj(
            xc, scale_c, shift_c, norm_g[l], w_in_bf[l], gq_q_g[l], gq_k_g[l], None, L + Lc, L, (ka, va, kg, vg))

        ya = _da_attention(da_lambda[l], qa, ka, va, ga, da_subln_g[l], lam_init, L + Lc, 0)
        ogq = _gq_attention(qg, kg, vg, L + Lc, 0)

        filt = (hy_w1[l], hy_b1[l], hy_w2[l], hy_b2[l], hy_w3[l], hy_b3[l], hy_w4[l], hy_freq[l])
        spec = _filter_spectrum(dft_lat[0], *_hyena_filter_eo(L, *filt))
        ohy = _hyena(uh, dft_lat, spec, hy_short_w[l], hy_short_b[l], hy_bias[l])

        if update_ctx:
            ya_c = _da_attention(da_lambda[l], qa_c, ka, va, ga_c, da_subln_g[l], lam_init, Lc, L // Lc)
            ogq_c = _gq_attention(qg_c, kg, vg, Lc, L // Lc)
            spec_c = _filter_spectrum(dft_ctx[0], *_hyena_filter_eo(Lc, *filt))
            ohy_c = _hyena(uh_c, dft_ctx, spec_c, hy_short_w[l], hy_short_b[l], hy_bias[l])
            xc = _outproj(ya_c, ogq_c, gg_c, ohy_c, gh_c, xc, gate_c, w_out_bf[l],
                          gq_out_g[l], hy_out_g[l], final_g, False)

        x = _outproj(ya, ogq, gg, ohy, gh, x, gate, w_out_bf[l],
                     gq_out_g[l], hy_out_g[l], final_g, l == depth - 1)
    return x
```

```python
import functools
import math

import jax
import jax.numpy as jnp
import numpy as np
from jax import lax
from jax.experimental import pallas as pl
from jax.experimental.pallas import tpu as pltpu

F32 = jnp.float32
BF16 = jnp.bfloat16

GRID_W = 64
EPS = 1e-6
ROPE_THETA = 10000.0
DA_HEADS = 4
DA_QK_DIM = 64
DA_V_DIM = 2 * DA_QK_DIM
DA_WIDTH = DA_HEADS * DA_V_DIM
GQ_HEADS = 8
GQ_KV_HEADS = 2
GQ_GROUP = GQ_HEADS // GQ_KV_HEADS
GQ_DIM = 128
GQ_WIDTH = GQ_HEADS * GQ_DIM
HY_WIDTH = 512
HY_EMB = 33
HY_BANDS = (HY_EMB - 1) // 2
HY_FFN = 64
HY_MAX_DECAY = math.log(1e-2) / 0.3
HY_MIN_DECAY = math.log(1e-2) / 1.5
HY_U = 3 * HY_WIDTH
D_MIX = DA_WIDTH + GQ_WIDTH + HY_WIDTH
COL_SIZES = (DA_WIDTH, DA_WIDTH, DA_WIDTH, DA_WIDTH,
             GQ_WIDTH, GQ_KV_HEADS * GQ_DIM, GQ_KV_HEADS * GQ_DIM, GQ_WIDTH,
             HY_U, HY_WIDTH)
COL_OFFS = tuple(int(v) for v in np.cumsum((0,) + COL_SIZES))
D_IN = COL_OFFS[-1]

LANE = 128
ATTN_CHAIN_ROWS = 512
ATTN_CHAINS = 16
ATTN_LOOKAHEAD = 1
DFT_FACTOR = 64
LOG2E = math.log2(math.e)
ONES_ROWS = 16
KEY_GROUP_ROWS = 256
VMEM_LIMIT = 56 * 1024 * 1024


def _cparams(*sem):
    return pltpu.CompilerParams(dimension_semantics=sem, vmem_limit_bytes=VMEM_LIMIT)


def _silu(g):
    return g / (1.0 + jnp.exp(-g))


def _rms_rows(x, g):
    return x * lax.rsqrt(jnp.mean(x * x, axis=-1, keepdims=True) + EPS) * g


def _mod_kernel(c_ref, w_ref, b_ref, o_ref):
    c = c_ref[...]
    o_ref[...] = jnp.dot(_silu(c), w_ref[...], preferred_element_type=F32,
                         precision=lax.Precision.HIGHEST) + b_ref[...]


def _modulation(c_all, ada_w, ada_b):
    depth, d, n3 = ada_w.shape
    r = c_all.shape[0]
    tn = 512
    return pl.pallas_call(
        _mod_kernel,
        grid=(depth, n3 // tn),
        in_specs=[pl.BlockSpec((r, d), lambda l, j: (0, 0)),
                  pl.BlockSpec((None, d, tn), lambda l, j: (l, 0, j)),
                  pl.BlockSpec((None, 1, tn), lambda l, j: (l, 0, j))],
        out_specs=pl.BlockSpec((None, r, tn), lambda l, j: (l, 0, j)),
        out_shape=jax.ShapeDtypeStruct((depth, r, n3), F32),
        compiler_params=_cparams("parallel", "parallel"),
        name="modulation",
    )(c_all, ada_w, ada_b.reshape(depth, 1, n3))


def _rope_tables(L, head_dim, reps):
    rows_n = L // GRID_W
    row = jnp.repeat(jnp.arange(rows_n), GRID_W).astype(F32)
    col = jnp.tile(jnp.arange(GRID_W), rows_n).astype(F32)
    axis_dim = head_dim // 2
    inv = ROPE_THETA ** (-jnp.arange(0, axis_dim, 2, dtype=F32) / axis_dim)
    ar, ac = row[:, None] * inv[None], col[:, None] * inv[None]
    z = jnp.zeros_like(ar)
    cos = jnp.concatenate([jnp.cos(ar), jnp.cos(ar), jnp.cos(ac), jnp.cos(ac)], axis=-1)
    s_lo = jnp.concatenate([-jnp.sin(ar), z, -jnp.sin(ac), z], axis=-1)
    s_hi = jnp.concatenate([z, jnp.sin(ar), z, jnp.sin(ac)], axis=-1)
    return tuple(jnp.tile(t, (1, reps)) for t in (cos, s_lo, s_hi))


def _rope_block(x, cos, s_lo, s_hi, half):
    fwd = pltpu.roll(x, LANE - half, axis=1)
    bwd = pltpu.roll(x, half, axis=1)
    return x * cos + fwd * s_lo + bwd * s_hi


def _inproj_kernel(*refs, rope, n_alias):
    x_ref, sc_ref, sh_ref, ng_ref, w_ref, qg_g_ref, kg_g_ref = refs[:7]
    n_in = 7 + (6 if rope else 0)
    tabs = refs[7:13] if rope else None
    qa_o, ka_o, va_o, ga_o, qg_o, kg_o, vg_o, gg_o, uh_o, gh_o = refs[n_in + n_alias:]

    x = x_ref[...]
    h = _rms_rows(x, ng_ref[...])
    h = h * (1.0 + sc_ref[...]) + sh_ref[...]
    hb = h.astype(BF16)

    def proj(group):
        off = COL_OFFS[group]
        return jnp.dot(hb, w_ref[:, off:off + COL_SIZES[group]], preferred_element_type=F32)

    def blocks(v):
        return [(slice(b * LANE, (b + 1) * LANE), v[:, b * LANE:(b + 1) * LANE])
                for b in range(v.shape[1] // LANE)]

    def rope_da(v):
        if not rope:
            return v
        return _rope_block(v, tabs[0][...], tabs[1][...], tabs[2][...], DA_QK_DIM // 4)

    def rope_gq(v):
        if not rope:
            return v
        return _rope_block(v, tabs[3][...], tabs[4][...], tabs[5][...], GQ_DIM // 4)

    for sl, v in blocks(proj(0)):
        qa_o[:, sl] = (rope_da(v) * (LOG2E * DA_QK_DIM ** -0.5)).astype(qa_o.dtype)
    for sl, v in blocks(proj(1)):
        ka_o[:, sl] = rope_da(v).astype(ka_o.dtype)
    va_o[...] = proj(2).T.astype(va_o.dtype)
    ga_o[...] = _silu(proj(3)).astype(ga_o.dtype)
    for sl, v in blocks(proj(4)):
        qg_o[:, sl] = (rope_gq(_rms_rows(v, qg_g_ref[...])) * (LOG2E * GQ_DIM ** -0.5)).astype(qg_o.dtype)
    for sl, v in blocks(proj(5)):
        kg_o[:, sl] = rope_gq(_rms_rows(v, kg_g_ref[...])).astype(kg_o.dtype)
    vg_o[...] = proj(6).T.astype(vg_o.dtype)
    gg_o[...] = _silu(proj(7)).astype(gg_o.dtype)
    uh_o[...] = proj(8).astype(uh_o.dtype)
    gh_o[...] = _silu(proj(9)).astype(gh_o.dtype)


def _inproj(xt, scale, shift, norm_g, w_bf, q_g, k_g, tabs, kv_rows, kv_row0, kv_bufs):
    B, T, D = xt.shape
    tm = min(512, T)
    assert kv_row0 % tm == 0
    rope = tabs is not None
    blk0 = kv_row0 // tm
    row = lambda b, i: (b, i, 0)
    krow = lambda b, i: (b, i + blk0, 0)
    vcol = lambda b, i: (b, 0, i + blk0)
    const2 = lambda b, i: (0, 0)
    in_specs = [pl.BlockSpec((None, tm, D), row),
                pl.BlockSpec((None, 1, D), lambda b, i: (b, 0, 0)),
                pl.BlockSpec((None, 1, D), lambda b, i: (b, 0, 0)),
                pl.BlockSpec((1, D), const2),
                pl.BlockSpec((D, D_IN), const2, pipeline_mode=pl.Buffered(1)),
                pl.BlockSpec((1, GQ_DIM), const2),
                pl.BlockSpec((1, GQ_DIM), const2)]
    args = [xt, scale, shift, norm_g.reshape(1, D), w_bf, q_g.reshape(1, GQ_DIM), k_g.reshape(1, GQ_DIM)]
    if rope:
        in_specs += [pl.BlockSpec((tm, LANE), lambda b, i: (i, 0))] * 6
        args += list(tabs)
    aliases = {}
    if kv_bufs is not None:
        kv_out_index = (1, 2, 5, 6)
        aliases = {len(args) + n: o for n, o in enumerate(kv_out_index)}
        in_specs += [pl.BlockSpec(memory_space=pl.ANY)] * len(kv_bufs)
        args += list(kv_bufs)
    kvw = GQ_KV_HEADS * GQ_DIM
    tok = lambda w, dt: (pl.BlockSpec((None, tm, w), row), jax.ShapeDtypeStruct((B, T, w), dt))
    keys = lambda w: (pl.BlockSpec((None, tm, w), krow), jax.ShapeDtypeStruct((B, kv_rows, w), BF16))
    vals = lambda w: (pl.BlockSpec((None, w, tm), vcol), jax.ShapeDtypeStruct((B, w, kv_rows), BF16))
    outs = [tok(DA_WIDTH, BF16), keys(DA_WIDTH), vals(DA_WIDTH), tok(DA_WIDTH, BF16),
            tok(GQ_WIDTH, BF16), keys(kvw), vals(kvw), tok(GQ_WIDTH, BF16),
            tok(HY_U, BF16), tok(HY_WIDTH, BF16)]
    return pl.pallas_call(
        functools.partial(_inproj_kernel, rope=rope, n_alias=len(aliases)),
        grid=(B, T // tm),
        in_specs=in_specs,
        out_specs=[o[0] for o in outs],
        out_shape=[o[1] for o in outs],
        input_output_aliases=aliases,
        compiler_params=_cparams("parallel", "parallel"),
        name="inproj_rope" if rope else "inproj_ctx",
    )(*args)


def _reduce_keys(a, op):
    tk, m = a.shape
    g = KEY_GROUP_ROWS if tk % KEY_GROUP_ROWS == 0 else tk
    part = op(a.reshape(tk // g, g, m), axis=0)
    return op(part, axis=0, keepdims=True)


def _scores_t(qs, k):
    return lax.dot_general(k, qs, (((1,), (1,)), ((), ())), preferred_element_type=F32)


def _softmax_pv_t(st, vt):
    m = _reduce_keys(st, jnp.max)
    p = jnp.exp2(st - m).astype(BF16)
    ot = jnp.dot(vt, p, preferred_element_type=F32)
    return (ot[:LANE] / ot[LANE:LANE + 1]).T


def _attend_chains(queries, k_ref, vt_ref):
    n = len(queries)
    vt = jnp.concatenate([vt_ref[...], jnp.ones((ONES_ROWS, vt_ref.shape[1]), BF16)], axis=0)
    scores = [_scores_t(queries[j](), k_ref[...]) for j in range(min(ATTN_LOOKAHEAD, n))]
    outs = []
    for i in range(n):
        if i + ATTN_LOOKAHEAD < n:
            scores.append(_scores_t(queries[i + ATTN_LOOKAHEAD](), k_ref[...]))
        outs.append(_softmax_pv_t(scores[i], vt))
        scores[i] = None
    return outs


def _da_kernel(lam_ref, q_ref, k_ref, v_ref, gate_ref, sg_ref, o_ref, *, lam_init):
    tq = q_ref.shape[0]
    lm = lam_ref[...]
    lam = (jnp.exp(jnp.sum(lm[0:1] * lm[1:2], axis=-1, keepdims=True))
           - jnp.exp(jnp.sum(lm[2:3] * lm[3:4], axis=-1, keepdims=True)) + lam_init)
    ch = min(ATTN_CHAIN_ROWS, tq)

    def component(r, c):
        def load():
            q = q_ref[r * ch:(r + 1) * ch, :]
            lane = lax.broadcasted_iota(jnp.int32, q.shape, 1)
            keep = (lane < DA_QK_DIM) if c == 0 else (lane >= DA_QK_DIM)
            return jnp.where(keep, q, jnp.zeros_like(q))
        return load

    n = tq // ch
    outs = _attend_chains([component(r, c) for r in range(n) for c in range(2)], k_ref, v_ref)
    for r in range(n):
        rows = slice(r * ch, (r + 1) * ch)
        o = _rms_rows(outs[2 * r] - lam * outs[2 * r + 1], sg_ref[...]) * (1.0 - lam_init)
        o_ref[rows, :] = (o * gate_ref[rows, :].astype(F32)).astype(o_ref.dtype)


def _da_attention(lam_p, q, k, v, gate, subln_g, lam_init, Tk, kblk):
    B, Tq, _ = q.shape
    tq = min(ATTN_CHAINS * ATTN_CHAIN_ROWS // 2, Tq)
    qmap = lambda b, h, i: (b, i, h)
    kmap = lambda b, h, i: (b, kblk, h)
    vmap = lambda b, h, i: (b, h, kblk)
    return pl.pallas_call(
        functools.partial(_da_kernel, lam_init=lam_init),
        grid=(B, DA_HEADS, Tq // tq),
        in_specs=[pl.BlockSpec((4, DA_QK_DIM), lambda b, h, i: (0, 0)),
                  pl.BlockSpec((None, tq, LANE), qmap),
                  pl.BlockSpec((None, Tk, LANE), kmap),
                  pl.BlockSpec((None, LANE, Tk), vmap),
                  pl.BlockSpec((None, tq, LANE), qmap),
                  pl.BlockSpec((1, LANE), lambda b, h, i: (0, 0))],
        out_specs=pl.BlockSpec((None, tq, LANE), qmap),
        out_shape=jax.ShapeDtypeStruct((B, Tq, DA_WIDTH), BF16),
        compiler_params=_cparams("parallel", "parallel", "arbitrary"),
        name="diff_attention",
    )(lam_p, q, k, v, gate, subln_g.reshape(1, DA_V_DIM))


def _gq_kernel(q_ref, k_ref, v_ref, o_ref):
    tq = q_ref.shape[0]
    ch = min(ATTN_CHAIN_ROWS, tq)
    where = [(slice(c * ch, (c + 1) * ch), slice(r * LANE, (r + 1) * LANE))
             for c in range(tq // ch) for r in range(GQ_GROUP)]
    outs = _attend_chains([functools.partial(lambda w: q_ref[w[0], w[1]], w) for w in where], k_ref, v_ref)
    for (rows, sl), o in zip(where, outs):
        o_ref[rows, sl] = o.astype(o_ref.dtype)


def _gq_attention(q, k, v, Tk, kblk):
    B, Tq, _ = q.shape
    tq = min(ATTN_CHAINS * ATTN_CHAIN_ROWS // GQ_GROUP, Tq)
    gw = GQ_GROUP * GQ_DIM
    qmap = lambda b, g, i: (b, i, g)
    kmap = lambda b, g, i: (b, kblk, g)
    vmap = lambda b, g, i: (b, g, kblk)
    return pl.pallas_call(
        _gq_kernel,
        grid=(B, GQ_KV_HEADS, Tq // tq),
        in_specs=[pl.BlockSpec((None, tq, gw), qmap),
                  pl.BlockSpec((None, Tk, LANE), kmap),
                  pl.BlockSpec((None, LANE, Tk), vmap)],
        out_specs=pl.BlockSpec((None, tq, gw), qmap),
        out_shape=jax.ShapeDtypeStruct((B, Tq, GQ_WIDTH), BF16),
        compiler_params=_cparams("parallel", "parallel", "arbitrary"),
        name="gqa_attention",
    )(q, k, v)


def _hy_pre_kernel(u0_ref, u1_ref, u2_ref, w0_ref, w1_ref, w2_ref, b0_ref, b1_ref, b2_ref,
                   x0_ref, z_ref):
    T = u0_ref.shape[0]
    row = lax.broadcasted_iota(jnp.int32, u0_ref.shape, 0)

    def sconv(u_ref, w_ref, b_ref):
        u = u_ref[...].astype(F32)
        w = w_ref[...]
        prev = jnp.where(row == 0, 0.0, pltpu.roll(u, 1, axis=0))
        nxt = jnp.where(row == T - 1, 0.0, pltpu.roll(u, T - 1, axis=0))
        return prev * w[0:1] + u * w[1:2] + nxt * w[2:3] + b_ref[...]

    x0_ref[...] = sconv(u0_ref, w0_ref, b0_ref).astype(x0_ref.dtype)
    z_ref[...] = (sconv(u2_ref, w2_ref, b2_ref) * sconv(u1_ref, w1_ref, b1_ref)).astype(z_ref.dtype)


def _hy_pre(uh, short_w, short_b):
    B, T, _ = uh.shape
    nb = HY_WIDTH // LANE
    u_specs = [pl.BlockSpec((None, T, LANE), lambda b, j, p=p: (b, 0, p * nb + j)) for p in range(3)]
    w_specs = [pl.BlockSpec((3, LANE), lambda b, j, p=p: (0, p * nb + j)) for p in range(3)]
    b_specs = [pl.BlockSpec((1, LANE), lambda b, j, p=p: (0, p * nb + j)) for p in range(3)]
    out_spec = pl.BlockSpec((None, T, LANE), lambda b, j: (b, 0, j))
    sb = short_b.reshape(1, HY_U)
    return pl.pallas_call(
        _hy_pre_kernel,
        grid=(B, nb),
        in_specs=u_specs + w_specs + b_specs,
        out_specs=[out_spec] * 2,
        out_shape=[jax.ShapeDtypeStruct((B, T, HY_WIDTH), BF16)] * 2,
        compiler_params=_cparams("parallel", "parallel"),
        name="hyena_short_conv",
    )(uh, uh, uh, short_w, short_w, short_w, sb, sb, sb)


def _filter_kernel(z_ref, w1_ref, b1_ref, w2_ref, b2_ref, w3_ref, b3_ref, w4f_ref, w4b_ref,
                   fr_ref, t_ref, dl_ref, fe_ref, fo_ref, h_ref):
    hp = lax.Precision.HIGHEST

    @pl.when(pl.program_id(0) == 0)
    def _():
        fr = fr_ref[...]
        h = jnp.sin(fr * (jnp.dot(z_ref[...], w1_ref[...], preferred_element_type=F32, precision=hp) + b1_ref[...]))
        h = jnp.sin(fr * (jnp.dot(h, w2_ref[...], preferred_element_type=F32, precision=hp) + b2_ref[...]))
        h_ref[...] = jnp.sin(fr * (jnp.dot(h, w3_ref[...], preferred_element_type=F32, precision=hp) + b3_ref[...]))

    h = h_ref[...]
    decay = jnp.exp(-t_ref[...] * jnp.abs(dl_ref[...]))
    fwd = jnp.dot(h, w4f_ref[...], preferred_element_type=F32, precision=hp) * decay
    bwd = jnp.dot(h, w4b_ref[...], preferred_element_type=F32, precision=hp) * decay
    row = lax.broadcasted_iota(jnp.int32, bwd.shape, 0)
    bwd = jnp.where(row == 0, 0.0, bwd)
    ss = jnp.sum(fwd * fwd + bwd * bwd, axis=0, keepdims=True)
    sc = lax.rsqrt(ss + EPS)
    fe_ref[...] = ((fwd + bwd) * sc).astype(fe_ref.dtype)
    fo_ref[...] = ((fwd - bwd) * sc).astype(fo_ref.dtype)


def _hyena_filter_eo(T, w1, b1, w2, b2, w3, b3, w4, freq):
    t = jnp.linspace(0.0, 1.0, T, dtype=F32)[:, None]
    w = 2.0 * math.pi * jnp.arange(T, dtype=F32)[:, None] / T
    f = jnp.linspace(1e-4, HY_BANDS - 1, HY_BANDS, dtype=F32)[None]
    z = jnp.concatenate([t, jnp.cos(f * w), -jnp.sin(f * w)], axis=-1)
    z = jnp.pad(z, ((0, 0), (0, LANE - HY_EMB)))
    w1p = jnp.pad(w1, ((0, LANE - HY_EMB), (0, 0)))
    deltas = jnp.linspace(HY_MIN_DECAY, HY_MAX_DECAY, HY_WIDTH, dtype=F32)[None]
    nb = HY_WIDTH // LANE
    c2 = lambda j: (0, 0)
    vec = lambda a: a.reshape(1, HY_FFN)
    return pl.pallas_call(
        _filter_kernel,
        grid=(nb,),
        in_specs=[pl.BlockSpec((T, LANE), c2),
                  pl.BlockSpec((LANE, HY_FFN), c2), pl.BlockSpec((1, HY_FFN), c2),
                  pl.BlockSpec((HY_FFN, HY_FFN), c2), pl.BlockSpec((1, HY_FFN), c2),
                  pl.BlockSpec((HY_FFN, HY_FFN), c2), pl.BlockSpec((1, HY_FFN), c2),
                  pl.BlockSpec((HY_FFN, LANE), lambda j: (0, j)),
                  pl.BlockSpec((HY_FFN, LANE), lambda j: (0, nb + j)),
                  pl.BlockSpec((1, HY_FFN), c2),
                  pl.BlockSpec((T, 1), c2),
                  pl.BlockSpec((1, LANE), lambda j: (0, j))],
        out_specs=[pl.BlockSpec((T, LANE), lambda j: (0, j))] * 2,
        out_shape=[jax.ShapeDtypeStruct((T, HY_WIDTH), BF16)] * 2,
        scratch_shapes=[pltpu.VMEM((T, HY_FFN), F32)],
        compiler_params=_cparams("arbitrary"),
        name="hyena_filter",
    )(z, w1p, vec(b1), w2, vec(b2), w3, vec(b3), w4, w4, vec(freq), t, deltas)


def _dft_tile_rows(T):
    return min(1024, 2 * T)


def _dft_tables(T):
    n = 2 * T
    tr = _dft_tile_rows(T)
    half = tr // 2
    f = DFT_FACTOR
    assert T % f == 0 and half % f == 0
    ang = 2.0 * math.pi / n

    def cs(k, s):
        a = ((k * s) % n).astype(F32) * ang
        return jnp.cos(a), jnp.sin(a)

    s = jnp.arange(T, dtype=jnp.int32)[None]
    cl, sl = cs(jnp.arange(f, dtype=jnp.int32)[:, None], s)
    ch, sh = cs(jnp.arange(T // f, dtype=jnp.int32)[:, None] * f, s)
    x = jnp.stack([cl, sl])[None, :, None]
    w = jnp.stack([-sl, cl])[None, :, None]
    ch5 = ch.reshape(n // tr, 1, half // f, 1, T)
    sh5 = sh.reshape(n // tr, 1, half // f, 1, T)
    a = (ch5 * x + sh5 * w).reshape(n, T).astype(BF16)
    nyq = jnp.where(jnp.arange(T) % 2 == 0, 1.0, -1.0).astype(BF16)
    a = a.at[half].set(nyq)

    r = np.arange(n)
    local = r % tr
    k = (r // tr) * half + local % half
    is_sin = local >= half
    is_nyq = is_sin & (k == 0)
    k = jnp.asarray(np.where(is_nyq, T, k), jnp.int32)[None]
    sel = jnp.asarray(is_sin & ~is_nyq)[None]
    c_lo, s_lo = cs(k, jnp.arange(f, dtype=jnp.int32)[:, None])
    c_hi, s_hi = cs(k, jnp.arange(T // f, dtype=jnp.int32)[:, None] * f)
    x_lo = jnp.where(sel, s_lo, c_lo)
    w_lo = jnp.where(sel, c_lo, -s_lo)
    at = (x_lo[None] * c_hi[:, None] + w_lo[None] * s_hi[:, None]).reshape(T, n).astype(BF16)
    return a, at


NYQ_ROWS = 16


def _dft_filter_kernel(a_ref, fe_ref, fo_ref, c_ref, s_ref, nyq_ref):
    half = a_ref.shape[0] // 2
    c_ref[...] = jnp.dot(a_ref[:half, :], fe_ref[...], preferred_element_type=F32)
    s_ref[...] = jnp.dot(a_ref[half:, :], fo_ref[...], preferred_element_type=F32)

    @pl.when(pl.program_id(0) == 0)
    def _():
        nyq_ref[...] = jnp.dot(a_ref[half:half + NYQ_ROWS, :], fe_ref[...], preferred_element_type=F32)


def _filter_spectrum(a, fe, fo):
    n, T = a.shape
    tr = _dft_tile_rows(T)
    half = tr // 2
    cosp, sinp, nyq = pl.pallas_call(
        _dft_filter_kernel,
        grid=(n // tr,),
        in_specs=[pl.BlockSpec((tr, T), lambda i: (i, 0)),
                  pl.BlockSpec((T, HY_WIDTH), lambda i: (0, 0)),
                  pl.BlockSpec((T, HY_WIDTH), lambda i: (0, 0))],
        out_specs=[pl.BlockSpec((half, HY_WIDTH), lambda i: (i, 0)),
                   pl.BlockSpec((half, HY_WIDTH), lambda i: (i, 0)),
                   pl.BlockSpec((NYQ_ROWS, HY_WIDTH), lambda i: (0, 0))],
        out_shape=[jax.ShapeDtypeStruct((T, HY_WIDTH), F32),
                   jax.ShapeDtypeStruct((T, HY_WIDTH), F32),
                   jax.ShapeDtypeStruct((NYQ_ROWS, HY_WIDTH), F32)],
        compiler_params=_cparams("arbitrary"),
        name="dft_filter",
    )(a, fe, fo)
    hr = cosp
    hi = -sinp
    hnyq = nyq[0:1]
    first = (jnp.arange(T) == 0)[:, None]
    ck = jnp.where(first, 1.0 / n, 2.0 / n)
    t1 = ck * hr
    t2 = jnp.where(first, 0.0, ck * hi)
    t4 = jnp.where(first, hnyq / n, ck * hr)
    return t1, t2, t4


def _dft_fwd_kernel(a_ref, z_ref, t1_ref, t2_ref, t4_ref, y_ref):
    half = a_ref.shape[0] // 2
    u = jnp.dot(a_ref[...], z_ref[...], preferred_element_type=F32)
    re, im = u[:half], u[half:]
    t2 = t2_ref[...]
    y_ref[:half, :] = (re * t1_ref[...] + im * t2).astype(y_ref.dtype)
    y_ref[half:, :] = (im * t4_ref[...] - re * t2).astype(y_ref.dtype)


def _dft_fwd(a, zb, t1, t2, t4):
    n, T = a.shape
    B = zb.shape[0]
    tr = _dft_tile_rows(T)
    tmap = lambda i, b: (i, 0)
    return pl.pallas_call(
        _dft_fwd_kernel,
        grid=(n // tr, B),
        in_specs=[pl.BlockSpec((tr, T), tmap),
                  pl.BlockSpec((None, T, HY_WIDTH), lambda i, b: (b, 0, 0)),
                  pl.BlockSpec((tr // 2, HY_WIDTH), tmap),
                  pl.BlockSpec((tr // 2, HY_WIDTH), tmap),
                  pl.BlockSpec((tr // 2, HY_WIDTH), tmap)],
        out_specs=pl.BlockSpec((None, tr, HY_WIDTH), lambda i, b: (b, i, 0)),
        out_shape=jax.ShapeDtypeStruct((B, n, HY_WIDTH), BF16),
        compiler_params=_cparams("parallel", "arbitrary"),
        name="dft_forward",
    )(a, zb, t1, t2, t4)


def _dft_inv_kernel(at_ref, y_ref, x0_ref, z_ref, skip_ref, o_ref):
    y = jnp.dot(at_ref[...], y_ref[...], preferred_element_type=F32)
    z = z_ref[...].astype(F32)
    o_ref[...] = (x0_ref[...].astype(F32) * (y + z * skip_ref[...])).astype(o_ref.dtype)


def _dft_inv(at, y, x0, z, skip):
    T, n = at.shape
    B = y.shape[0]
    tt = min(512, T)
    rmap = lambda i, b: (b, i, 0)
    return pl.pallas_call(
        _dft_inv_kernel,
        grid=(T // tt, B),
        in_specs=[pl.BlockSpec((tt, n), lambda i, b: (i, 0)),
                  pl.BlockSpec((None, n, HY_WIDTH), lambda i, b: (b, 0, 0)),
                  pl.BlockSpec((None, tt, HY_WIDTH), rmap),
                  pl.BlockSpec((None, tt, HY_WIDTH), rmap),
                  pl.BlockSpec((1, HY_WIDTH), lambda i, b: (0, 0))],
        out_specs=pl.BlockSpec((None, tt, HY_WIDTH), rmap),
        out_shape=jax.ShapeDtypeStruct((B, T, HY_WIDTH), BF16),
        compiler_params=_cparams("parallel", "arbitrary"),
        name="dft_inverse",
    )(at, y, x0, z, skip.reshape(1, HY_WIDTH))


def _hyena(uh, dft, spectrum, short_w, short_b, skip):
    a, at = dft
    x0, z = _hy_pre(uh, short_w, short_b)
    y = _dft_fwd(a, z, *spectrum)
    return _dft_inv(at, y, x0, z, skip)


def _outproj_kernel(ya_ref, ogq_ref, gg_ref, ohy_ref, gh_ref, x_ref, gate_ref, w_ref,
                    gqg_ref, hyg_ref, fg_ref, o_ref, *, final):
    yg = _rms_rows(ogq_ref[...].astype(F32), gqg_ref[...]) * gg_ref[...].astype(F32)
    yh = _rms_rows(ohy_ref[...].astype(F32), hyg_ref[...]) * gh_ref[...].astype(F32)
    a0, a1, a2 = DA_WIDTH, DA_WIDTH + GQ_WIDTH, D_MIX
    y = jnp.dot(ya_ref[...], w_ref[0:a0, :], preferred_element_type=F32)
    y = y + jnp.dot(yg.astype(BF16), w_ref[a0:a1, :], preferred_element_type=F32)
    y = y + jnp.dot(yh.astype(BF16), w_ref[a1:a2, :], preferred_element_type=F32)
    out = x_ref[...] + gate_ref[...] * y
    if final:
        out = _rms_rows(out, fg_ref[...])
    o_ref[...] = out


def _outproj(ya, ogq, gg, ohy, gh, xt, gate, w_bf, gq_out_g, hy_out_g, final_g, final):
    B, T, D = xt.shape
    tm = min(512, T)
    row = lambda b, i: (b, i, 0)
    c2 = lambda b, i: (0, 0)
    return pl.pallas_call(
        functools.partial(_outproj_kernel, final=final),
        grid=(B, T // tm),
        in_specs=[pl.BlockSpec((None, tm, DA_WIDTH), row),
                  pl.BlockSpec((None, tm, GQ_WIDTH), row),
                  pl.BlockSpec((None, tm, GQ_WIDTH), row),
                  pl.BlockSpec((None, tm, HY_WIDTH), row),
                  pl.BlockSpec((None, tm, HY_WIDTH), row),
                  pl.BlockSpec((None, tm, D), row),
                  pl.BlockSpec((None, 1, D), lambda b, i: (b, 0, 0)),
                  pl.BlockSpec((D_MIX, D), c2),
                  pl.BlockSpec((1, GQ_WIDTH), c2),
                  pl.BlockSpec((1, HY_WIDTH), c2),
                  pl.BlockSpec((1, D), c2)],
        out_specs=pl.BlockSpec((None, tm, D), row),
        out_shape=jax.ShapeDtypeStruct((B, T, D), F32),
        compiler_params=_cparams("parallel", "parallel"),
        name="outproj_final" if final else "outproj",
    )(ya, ogq, gg, ohy, gh, xt, gate, w_bf, gq_out_g.reshape(1, GQ_WIDTH),
      hy_out_g.reshape(1, HY_WIDTH), final_g.reshape(1, D))


def kernel(x, c, ctx, c_ctx, ada_w, ada_b, norm_g, w_in, w_out, da_lambda, da_subln_g, gq_q_g, gq_k_g, gq_out_g, hy_short_w, hy_short_b, hy_w1, hy_b1, hy_w2, hy_b2, hy_w3, hy_b3, hy_w4, hy_freq, hy_bias, hy_out_g, final_g):
    B, L, D = x.shape
    Lc = ctx.shape[1]
    depth = ada_w.shape[0]
    assert L % GRID_W == 0 and L % Lc == 0 and D_IN == w_in.shape[2]

    tabs = _rope_tables(L, DA_QK_DIM, 2) + _rope_tables(L, GQ_DIM, 1)
    dft_lat = _dft_tables(L)
    dft_ctx = _dft_tables(Lc)

    n_mod = B + 1
    pad = (-n_mod) % 8
    c_all = jnp.concatenate([c, c_ctx[None], jnp.zeros((pad, D), F32)], axis=0)
    mod = _modulation(c_all, ada_w, ada_b)

    w_in_bf = w_in.astype(BF16)
    w_out_bf = w_out.astype(BF16)

    xc = ctx
    for l in range(depth):
        update_ctx = l < depth - 1
        lam_init = 0.8 - 0.6 * math.exp(-0.3 * l)
        shift, scale, gate = (mod[l, :B, i * D:(i + 1) * D][:, None, :] for i in range(3))
        shift_c, scale_c, gate_c = (jnp.broadcast_to(mod[l, B:B + 1, i * D:(i + 1) * D][:, None, :], (B, 1, D))
                                    for i in range(3))

        qa, ka, va, ga, qg, kg, vg, gg, uh, gh = _inproj(
            x, scale, shift, norm_g[l], w_in_bf[l], gq_q_g[l], gq_k_g[l], tabs, L + Lc, 0, None)
        qa_c, ka, va, ga_c, qg_c, kg, vg, gg_c, uh_c, gh_c = _inproj(
            xc, scale_c, shift_c, norm_g[l], w_in_bf[l], gq_q_g[l], gq_k_g[l], None, L + Lc, L, (ka, va, kg, vg))

        ya = _da_attention(da_lambda[l], qa, ka, va, ga, da_subln_g[l], lam_init, L + Lc, 0)
        ogq = _gq_attention(qg, kg, vg, L + Lc, 0)

        filt = (hy_w1[l], hy_b1[l], hy_w2[l], hy_b2[l], hy_w3[l], hy_b3[l], hy_w4[l], hy_freq[l])
        spec = _filter_spectrum(dft_lat[0], *_hyena_filter_eo(L, *filt))
        ohy = _hyena(uh, dft_lat, spec, hy_short_w[l], hy_short_b[l], hy_bias[l])

        if update_ctx:
            ya_c = _da_attention(da_lambda[l], qa_c, ka, va, ga_c, da_subln_g[l], lam_init, Lc, L // Lc)
            ogq_c = _gq_attention(qg_c, kg, vg, Lc, L // Lc)
            spec_c = _filter_spectrum(dft_ctx[0], *_hyena_filter_eo(Lc, *filt))
            ohy_c = _hyena(uh_c, dft_ctx, spec_c, hy_short_w[l], hy_short_b[l], hy_bias[l])
            xc = _outproj(ya_c, ogq_c, gg_c, ohy_c, gh_c, xc, gate_c, w_out_bf[l],
                          gq_out_g[l], hy_out_g[l], final_g, False)

        x = _outproj(ya, ogq, gg, ohy, gh, x, gate, w_out_bf[l],
                     gq_out_g[l], hy_out_g[l], final_g, l == depth - 1)
    return x
```

```python
import functools
import math

import jax
import jax.numpy as jnp
import numpy as np
from jax import lax
from jax.experimental import pallas as pl
from jax.experimental.pallas import tpu as pltpu

F32 = jnp.float32
BF16 = jnp.bfloat16

GRID_W = 64
EPS = 1e-6
ROPE_THETA = 10000.0
DA_HEADS = 4
DA_QK_DIM = 64
DA_V_DIM = 2 * DA_QK_DIM
DA_WIDTH = DA_HEADS * DA_V_DIM
GQ_HEADS = 8
GQ_KV_HEADS = 2
GQ_GROUP = GQ_HEADS // GQ_KV_HEADS
GQ_DIM = 128
GQ_WIDTH = GQ_HEADS * GQ_DIM
HY_WIDTH = 512
HY_EMB = 33
HY_BANDS = (HY_EMB - 1) // 2
HY_FFN = 64
HY_MAX_DECAY = math.log(1e-2) / 0.3
HY_MIN_DECAY = math.log(1e-2) / 1.5
HY_U = 3 * HY_WIDTH
D_MIX = DA_WIDTH + GQ_WIDTH + HY_WIDTH
COL_SIZES = (DA_WIDTH, DA_WIDTH, DA_WIDTH, DA_WIDTH,
             GQ_WIDTH, GQ_KV_HEADS * GQ_DIM, GQ_KV_HEADS * GQ_DIM, GQ_WIDTH,
             HY_U, HY_WIDTH)
COL_OFFS = tuple(int(v) for v in np.cumsum((0,) + COL_SIZES))
D_IN = COL_OFFS[-1]

LANE = 128
ATTN_CHAIN_ROWS = 512
ATTN_CHAINS = 8
ATTN_LOOKAHEAD = 1
DFT_FACTOR = 64
LOG2E = math.log2(math.e)
ONES_ROWS = 16
KEY_GROUP_ROWS = 256
VMEM_LIMIT = 56 * 1024 * 1024


def _cparams(*sem):
    return pltpu.CompilerParams(dimension_semantics=sem, vmem_limit_bytes=VMEM_LIMIT)


def _silu(g):
    return g / (1.0 + jnp.exp(-g))


def _rms_rows(x, g):
    return x * lax.rsqrt(jnp.mean(x * x, axis=-1, keepdims=True) + EPS) * g


def _mod_kernel(c_ref, w_ref, b_ref, o_ref):
    c = c_ref[...]
    o_ref[...] = jnp.dot(_silu(c), w_ref[...], preferred_element_type=F32,
                         precision=lax.Precision.HIGHEST) + b_ref[...]


def _modulation(c_all, ada_w, ada_b):
    depth, d, n3 = ada_w.shape
    r = c_all.shape[0]
    tn = 512
    return pl.pallas_call(
        _mod_kernel,
        grid=(depth, n3 // tn),
        in_specs=[pl.BlockSpec((r, d), lambda l, j: (0, 0)),
                  pl.BlockSpec((None, d, tn), lambda l, j: (l, 0, j)),
                  pl.BlockSpec((None, 1, tn), lambda l, j: (l, 0, j))],
        out_specs=pl.BlockSpec((None, r, tn), lambda l, j: (l, 0, j)),
        out_shape=jax.ShapeDtypeStruct((depth, r, n3), F32),
        compiler_params=_cparams("parallel", "parallel"),
        name="modulation",
    )(c_all, ada_w, ada_b.reshape(depth, 1, n3))


def _rope_tables(L, head_dim, reps):
    rows_n = L // GRID_W
    row = jnp.repeat(jnp.arange(rows_n), GRID_W).astype(F32)
    col = jnp.tile(jnp.arange(GRID_W), rows_n).astype(F32)
    axis_dim = head_dim // 2
    inv = ROPE_THETA ** (-jnp.arange(0, axis_dim, 2, dtype=F32) / axis_dim)
    ar, ac = row[:, None] * inv[None], col[:, None] * inv[None]
    z = jnp.zeros_like(ar)
    cos = jnp.concatenate([jnp.cos(ar), jnp.cos(ar), jnp.cos(ac), jnp.cos(ac)], axis=-1)
    s_lo = jnp.concatenate([-jnp.sin(ar), z, -jnp.sin(ac), z], axis=-1)
    s_hi = jnp.concatenate([z, jnp.sin(ar), z, jnp.sin(ac)], axis=-1)
    return tuple(jnp.tile(t, (1, reps)) for t in (cos, s_lo, s_hi))


def _rope_block(x, cos, s_lo, s_hi, half):
    fwd = pltpu.roll(x, LANE - half, axis=1)
    bwd = pltpu.roll(x, half, axis=1)
    return x * cos + fwd * s_lo + bwd * s_hi


def _inproj_kernel(*refs, rope, n_alias):
    x_ref, sc_ref, sh_ref, ng_ref, w_ref, qg_g_ref, kg_g_ref = refs[:7]
    n_in = 7 + (6 if rope else 0)
    tabs = refs[7:13] if rope else None
    qa_o, ka_o, va_o, ga_o, qg_o, kg_o, vg_o, gg_o, uh_o, gh_o = refs[n_in + n_alias:]

    x = x_ref[...]
    h = _rms_rows(x, ng_ref[...])
    h = h * (1.0 + sc_ref[...]) + sh_ref[...]
    hb = h.astype(BF16)

    def proj(group):
        off = COL_OFFS[group]
        return jnp.dot(hb, w_ref[:, off:off + COL_SIZES[group]], preferred_element_type=F32)

    def blocks(v):
        return [(slice(b * LANE, (b + 1) * LANE), v[:, b * LANE:(b + 1) * LANE])
                for b in range(v.shape[1] // LANE)]

    def rope_da(v):
        if not rope:
            return v
        return _rope_block(v, tabs[0][...], tabs[1][...], tabs[2][...], DA_QK_DIM // 4)

    def rope_gq(v):
        if not rope:
            return v
        return _rope_block(v, tabs[3][...], tabs[4][...], tabs[5][...], GQ_DIM // 4)

    for sl, v in blocks(proj(0)):
        qa_o[:, sl] = (rope_da(v) * (LOG2E * DA_QK_DIM ** -0.5)).astype(qa_o.dtype)
    for sl, v in blocks(proj(1)):
        ka_o[:, sl] = rope_da(v).astype(ka_o.dtype)
    va_o[...] = proj(2).T.astype(va_o.dtype)
    ga_o[...] = _silu(proj(3)).astype(ga_o.dtype)
    for sl, v in blocks(proj(4)):
        qg_o[:, sl] = (rope_gq(_rms_rows(v, qg_g_ref[...])) * (LOG2E * GQ_DIM ** -0.5)).astype(qg_o.dtype)
    for sl, v in blocks(proj(5)):
        kg_o[:, sl] = rope_gq(_rms_rows(v, kg_g_ref[...])).astype(kg_o.dtype)
    vg_o[...] = proj(6).T.astype(vg_o.dtype)
    gg_o[...] = _silu(proj(7)).astype(gg_o.dtype)
    uh_o[...] = proj(8).astype(uh_o.dtype)
    gh_o[...] = _silu(proj(9)).astype(gh_o.dtype)


def _inproj(xt, scale, shift, norm_g, w_bf, q_g, k_g, tabs, kv_rows, kv_row0, kv_bufs):
    B, T, D = xt.shape
    tm = min(512, T)
    assert kv_row0 % tm == 0
    rope = tabs is not None
    blk0 = kv_row0 // tm
    row = lambda b, i: (b, i, 0)
    krow = lambda b, i: (b, i + blk0, 0)
    vcol = lambda b, i: (b, 0, i + blk0)
    const2 = lambda b, i: (0, 0)
    in_specs = [pl.BlockSpec((None, tm, D), row),
                pl.BlockSpec((None, 1, D), lambda b, i: (b, 0, 0)),
                pl.BlockSpec((None, 1, D), lambda b, i: (b, 0, 0)),
                pl.BlockSpec((1, D), const2),
                pl.BlockSpec((D, D_IN), const2, pipeline_mode=pl.Buffered(1)),
                pl.BlockSpec((1, GQ_DIM), const2),
                pl.BlockSpec((1, GQ_DIM), const2)]
    args = [xt, scale, shift, norm_g.reshape(1, D), w_bf, q_g.reshape(1, GQ_DIM), k_g.reshape(1, GQ_DIM)]
    if rope:
        in_specs += [pl.BlockSpec((tm, LANE), lambda b, i: (i, 0))] * 6
        args += list(tabs)
    aliases = {}
    if kv_bufs is not None:
        kv_out_index = (1, 2, 5, 6)
        aliases = {len(args) + n: o for n, o in enumerate(kv_out_index)}
        in_specs += [pl.BlockSpec(memory_space=pl.ANY)] * len(kv_bufs)
        args += list(kv_bufs)
    kvw = GQ_KV_HEADS * GQ_DIM
    tok = lambda w, dt: (pl.BlockSpec((None, tm, w), row), jax.ShapeDtypeStruct((B, T, w), dt))
    keys = lambda w: (pl.BlockSpec((None, tm, w), krow), jax.ShapeDtypeStruct((B, kv_rows, w), BF16))
    vals = lambda w: (pl.BlockSpec((None, w, tm), vcol), jax.ShapeDtypeStruct((B, w, kv_rows), BF16))
    outs = [tok(DA_WIDTH, BF16), keys(DA_WIDTH), vals(DA_WIDTH), tok(DA_WIDTH, BF16),
            tok(GQ_WIDTH, BF16), keys(kvw), vals(kvw), tok(GQ_WIDTH, BF16),
            tok(HY_U, BF16), tok(HY_WIDTH, BF16)]
    return pl.pallas_call(
        functools.partial(_inproj_kernel, rope=rope, n_alias=len(aliases)),
        grid=(B, T // tm),
        in_specs=in_specs,
        out_specs=[o[0] for o in outs],
        out_shape=[o[1] for o in outs],
        input_output_aliases=aliases,
        compiler_params=_cparams("parallel", "parallel"),
        name="inproj_rope" if rope else "inproj_ctx",
    )(*args)


def _reduce_keys(a, op):
    tk, m = a.shape
    g = KEY_GROUP_ROWS if tk % KEY_GROUP_ROWS == 0 else tk
    part = op(a.reshape(tk // g, g, m), axis=0)
    return op(part, axis=0, keepdims=True)


def _scores_t(qs, k):
    return lax.dot_general(k, qs, (((1,), (1,)), ((), ())), preferred_element_type=F32)


def _softmax_pv_t(st, vt):
    m = _reduce_keys(st, jnp.max)
    p = jnp.exp2(st - m).astype(BF16)
    ot = jnp.dot(vt, p, preferred_element_type=F32)
    return (ot[:LANE] / ot[LANE:LANE + 1]).T


def _attend_chains(queries, k_ref, vt_ref):
    n = len(queries)
    vt = jnp.concatenate([vt_ref[...], jnp.ones((ONES_ROWS, vt_ref.shape[1]), BF16)], axis=0)
    scores = [_scores_t(queries[j](), k_ref[...]) for j in range(min(ATTN_LOOKAHEAD, n))]
    outs = []
    for i in range(n):
        if i + ATTN_LOOKAHEAD < n:
            scores.append(_scores_t(queries[i + ATTN_LOOKAHEAD](), k_ref[...]))
        outs.append(_softmax_pv_t(scores[i], vt))
        scores[i] = None
    return outs


def _da_kernel(lam_ref, q_ref, k_ref, v_ref, gate_ref, sg_ref, o_ref, *, lam_init):
    tq = q_ref.shape[0]
    lm = lam_ref[...]
    lam = (jnp.exp(jnp.sum(lm[0:1] * lm[1:2], axis=-1, keepdims=True))
           - jnp.exp(jnp.sum(lm[2:3] * lm[3:4], axis=-1, keepdims=True)) + lam_init)
    ch = min(ATTN_CHAIN_ROWS, tq)

    def component(r, c):
        def load():
            q = q_ref[r * ch:(r + 1) * ch, :]
            lane = lax.broadcasted_iota(jnp.int32, q.shape, 1)
            keep = (lane < DA_QK_DIM) if c == 0 else (lane >= DA_QK_DIM)
            return jnp.where(keep, q, jnp.zeros_like(q))
        return load

    n = tq // ch
    outs = _attend_chains([component(r, c) for r in range(n) for c in range(2)], k_ref, v_ref)
    for r in range(n):
        rows = slice(r * ch, (r + 1) * ch)
        o = _rms_rows(outs[2 * r] - lam * outs[2 * r + 1], sg_ref[...]) * (1.0 - lam_init)
        o_ref[rows, :] = (o * gate_ref[rows, :].astype(F32)).astype(o_ref.dtype)


def _da_attention(lam_p, q, k, v, gate, subln_g, lam_init, Tk, kblk):
    B, Tq, _ = q.shape
    tq = min(ATTN_CHAINS * ATTN_CHAIN_ROWS // 2, Tq)
    qmap = lambda b, h, i: (b, i, h)
    kmap = lambda b, h, i: (b, kblk, h)
    vmap = lambda b, h, i: (b, h, kblk)
    return pl.pallas_call(
        functools.partial(_da_kernel, lam_init=lam_init),
        grid=(B, DA_HEADS, Tq // tq),
        in_specs=[pl.BlockSpec((4, DA_QK_DIM), lambda b, h, i: (0, 0)),
                  pl.BlockSpec((None, tq, LANE), qmap),
                  pl.BlockSpec((None, Tk, LANE), kmap),
                  pl.BlockSpec((None, LANE, Tk), vmap),
                  pl.BlockSpec((None, tq, LANE), qmap),
                  pl.BlockSpec((1, LANE), lambda b, h, i: (0, 0))],
        out_specs=pl.BlockSpec((None, tq, LANE), qmap),
        out_shape=jax.ShapeDtypeStruct((B, Tq, DA_WIDTH), BF16),
        compiler_params=_cparams("parallel", "parallel", "arbitrary"),
        name="diff_attention",
    )(lam_p, q, k, v, gate, subln_g.reshape(1, DA_V_DIM))


def _gq_kernel(q_ref, k_ref, v_ref, o_ref):
    tq = q_ref.shape[0]
    ch = min(ATTN_CHAIN_ROWS, tq)
    where = [(slice(c * ch, (c + 1) * ch), slice(r * LANE, (r + 1) * LANE))
             for c in range(tq // ch) for r in range(GQ_GROUP)]
    outs = _attend_chains([functools.partial(lambda w: q_ref[w[0], w[1]], w) for w in where], k_ref, v_ref)
    for (rows, sl), o in zip(where, outs):
        o_ref[rows, sl] = o.astype(o_ref.dtype)


def _gq_attention(q, k, v, Tk, kblk):
    B, Tq, _ = q.shape
    tq = min(ATTN_CHAINS * ATTN_CHAIN_ROWS // GQ_GROUP, Tq)
    gw = GQ_GROUP * GQ_DIM
    qmap = lambda b, g, i: (b, i, g)
    kmap = lambda b, g, i: (b, kblk, g)
    vmap = lambda b, g, i: (b, g, kblk)
    return pl.pallas_call(
        _gq_kernel,
        grid=(B, GQ_KV_HEADS, Tq // tq),
        in_specs=[pl.BlockSpec((None, tq, gw), qmap),
                  pl.BlockSpec((None, Tk, LANE), kmap),
                  pl.BlockSpec((None, LANE, Tk), vmap)],
        out_specs=pl.BlockSpec((None, tq, gw), qmap),
        out_shape=jax.ShapeDtypeStruct((B, Tq, GQ_WIDTH), BF16),
        compiler_params=_cparams("parallel", "parallel", "arbitrary"),
        name="gqa_attention",
    )(q, k, v)


def _hy_pre_kernel(u0_ref, u1_ref, u2_ref, w0_ref, w1_ref, w2_ref, b0_ref, b1_ref, b2_ref,
                   x0_ref, z_ref):
    T = u0_ref.shape[0]
    row = lax.broadcasted_iota(jnp.int32, u0_ref.shape, 0)

    def sconv(u_ref, w_ref, b_ref):
        u = u_ref[...].astype(F32)
        w = w_ref[...]
        prev = jnp.where(row == 0, 0.0, pltpu.roll(u, 1, axis=0))
        nxt = jnp.where(row == T - 1, 0.0, pltpu.roll(u, T - 1, axis=0))
        return prev * w[0:1] + u * w[1:2] + nxt * w[2:3] + b_ref[...]

    x0_ref[...] = sconv(u0_ref, w0_ref, b0_ref).astype(x0_ref.dtype)
    z_ref[...] = (sconv(u2_ref, w2_ref, b2_ref) * sconv(u1_ref, w1_ref, b1_ref)).astype(z_ref.dtype)


def _hy_pre(uh, short_w, short_b):
    B, T, _ = uh.shape
    nb = HY_WIDTH // LANE
    u_specs = [pl.BlockSpec((None, T, LANE), lambda b, j, p=p: (b, 0, p * nb + j)) for p in range(3)]
    w_specs = [pl.BlockSpec((3, LANE), lambda b, j, p=p: (0, p * nb + j)) for p in range(3)]
    b_specs = [pl.BlockSpec((1, LANE), lambda b, j, p=p: (0, p * nb + j)) for p in range(3)]
    out_spec = pl.BlockSpec((None, T, LANE), lambda b, j: (b, 0, j))
    sb = short_b.reshape(1, HY_U)
    return pl.pallas_call(
        _hy_pre_kernel,
        grid=(B, nb),
        in_specs=u_specs + w_specs + b_specs,
        out_specs=[out_spec] * 2,
        out_shape=[jax.ShapeDtypeStruct((B, T, HY_WIDTH), BF16)] * 2,
        compiler_params=_cparams("parallel", "parallel"),
        name="hyena_short_conv",
    )(uh, uh, uh, short_w, short_w, short_w, sb, sb, sb)


def _filter_kernel(z_ref, w1_ref, b1_ref, w2_ref, b2_ref, w3_ref, b3_ref, w4f_ref, w4b_ref,
                   fr_ref, t_ref, dl_ref, fe_ref, fo_ref, h_ref):
    hp = lax.Precision.HIGHEST

    @pl.when(pl.program_id(0) == 0)
    def _():
        fr = fr_ref[...]
        h = jnp.sin(fr * (jnp.dot(z_ref[...], w1_ref[...], preferred_element_type=F32, precision=hp) + b1_ref[...]))
        h = jnp.sin(fr * (jnp.dot(h, w2_ref[...], preferred_element_type=F32, precision=hp) + b2_ref[...]))
        h_ref[...] = jnp.sin(fr * (jnp.dot(h, w3_ref[...], preferred_element_type=F32, precision=hp) + b3_ref[...]))

    h = h_ref[...]
    decay = jnp.exp(-t_ref[...] * jnp.abs(dl_ref[...]))
    fwd = jnp.dot(h, w4f_ref[...], preferred_element_type=F32, precision=hp) * decay
    bwd = jnp.dot(h, w4b_ref[...], preferred_element_type=F32, precision=hp) * decay
    row = lax.broadcasted_iota(jnp.int32, bwd.shape, 0)
    bwd = jnp.where(row == 0, 0.0, bwd)
    ss = jnp.sum(fwd * fwd + bwd * bwd, axis=0, keepdims=True)
    sc = lax.rsqrt(ss + EPS)
    fe_ref[...] = ((fwd + bwd) * sc).astype(fe_ref.dtype)
    fo_ref[...] = ((fwd - bwd) * sc).astype(fo_ref.dtype)


def _hyena_filter_eo(T, w1, b1, w2, b2, w3, b3, w4, freq):
    t = jnp.linspace(0.0, 1.0, T, dtype=F32)[:, None]
    w = 2.0 * math.pi * jnp.arange(T, dtype=F32)[:, None] / T
    f = jnp.linspace(1e-4, HY_BANDS - 1, HY_BANDS, dtype=F32)[None]
    z = jnp.concatenate([t, jnp.cos(f * w), -jnp.sin(f * w)], axis=-1)
    z = jnp.pad(z, ((0, 0), (0, LANE - HY_EMB)))
    w1p = jnp.pad(w1, ((0, LANE - HY_EMB), (0, 0)))
    deltas = jnp.linspace(HY_MIN_DECAY, HY_MAX_DECAY, HY_WIDTH, dtype=F32)[None]
    nb = HY_WIDTH // LANE
    c2 = lambda j: (0, 0)
    vec = lambda a: a.reshape(1, HY_FFN)
    return pl.pallas_call(
        _filter_kernel,
        grid=(nb,),
        in_specs=[pl.BlockSpec((T, LANE), c2),
                  pl.BlockSpec((LANE, HY_FFN), c2), pl.BlockSpec((1, HY_FFN), c2),
                  pl.BlockSpec((HY_FFN, HY_FFN), c2), pl.BlockSpec((1, HY_FFN), c2),
                  pl.BlockSpec((HY_FFN, HY_FFN), c2), pl.BlockSpec((1, HY_FFN), c2),
                  pl.BlockSpec((HY_FFN, LANE), lambda j: (0, j)),
                  pl.BlockSpec((HY_FFN, LANE), lambda j: (0, nb + j)),
                  pl.BlockSpec((1, HY_FFN), c2),
                  pl.BlockSpec((T, 1), c2),
                  pl.BlockSpec((1, LANE), lambda j: (0, j))],
        out_specs=[pl.BlockSpec((T, LANE), lambda j: (0, j))] * 2,
        out_shape=[jax.ShapeDtypeStruct((T, HY_WIDTH), BF16)] * 2,
        scratch_shapes=[pltpu.VMEM((T, HY_FFN), F32)],
        compiler_params=_cparams("arbitrary"),
        name="hyena_filter",
    )(z, w1p, vec(b1), w2, vec(b2), w3, vec(b3), w4, w4, vec(freq), t, deltas)


def _dft_tile_rows(T):
    return min(1024, 2 * T)


def _dft_tables(T):
    n = 2 * T
    tr = _dft_tile_rows(T)
    half = tr // 2
    f = DFT_FACTOR
    assert T % f == 0 and half % f == 0
    ang = 2.0 * math.pi / n

    def cs(k, s):
        a = ((k * s) % n).astype(F32) * ang
        return jnp.cos(a), jnp.sin(a)

    s = jnp.arange(T, dtype=jnp.int32)[None]
    cl, sl = cs(jnp.arange(f, dtype=jnp.int32)[:, None], s)
    ch, sh = cs(jnp.arange(T // f, dtype=jnp.int32)[:, None] * f, s)
    x = jnp.stack([cl, sl])[None, :, None]
    w = jnp.stack([-sl, cl])[None, :, None]
    ch5 = ch.reshape(n // tr, 1, half // f, 1, T)
    sh5 = sh.reshape(n // tr, 1, half // f, 1, T)
    a = (ch5 * x + sh5 * w).reshape(n, T).astype(BF16)
    nyq = jnp.where(jnp.arange(T) % 2 == 0, 1.0, -1.0).astype(BF16)
    a = a.at[half].set(nyq)

    r = np.arange(n)
    local = r % tr
    k = (r // tr) * half + local % half
    is_sin = local >= half
    is_nyq = is_sin & (k == 0)
    k = jnp.asarray(np.where(is_nyq, T, k), jnp.int32)[None]
    sel = jnp.asarray(is_sin & ~is_nyq)[None]
    c_lo, s_lo = cs(k, jnp.arange(f, dtype=jnp.int32)[:, None])
    c_hi, s_hi = cs(k, jnp.arange(T // f, dtype=jnp.int32)[:, None] * f)
    x_lo = jnp.where(sel, s_lo, c_lo)
    w_lo = jnp.where(sel, c_lo, -s_lo)
    at = (x_lo[None] * c_hi[:, None] + w_lo[None] * s_hi[:, None]).reshape(T, n).astype(BF16)
    return a, at


NYQ_ROWS = 16


def _dft_filter_kernel(a_ref, fe_ref, fo_ref, c_ref, s_ref, nyq_ref):
    half = a_ref.shape[0] // 2
    c_ref[...] = jnp.dot(a_ref[:half, :], fe_ref[...], preferred_element_type=F32)
    s_ref[...] = jnp.dot(a_ref[half:, :], fo_ref[...], preferred_element_type=F32)

    @pl.when(pl.program_id(0) == 0)
    def _():
        nyq_ref[...] = jnp.dot(a_ref[half:half + NYQ_ROWS, :], fe_ref[...], preferred_element_type=F32)


def _filter_spectrum(a, fe, fo):
    n, T = a.shape
    tr = _dft_tile_rows(T)
    half = tr // 2
    cosp, sinp, nyq = pl.pallas_call(
        _dft_filter_kernel,
        grid=(n // tr,),
        in_specs=[pl.BlockSpec((tr, T), lambda i: (i, 0)),
                  pl.BlockSpec((T, HY_WIDTH), lambda i: (0, 0)),
                  pl.BlockSpec((T, HY_WIDTH), lambda i: (0, 0))],
        out_specs=[pl.BlockSpec((half, HY_WIDTH), lambda i: (i, 0)),
                   pl.BlockSpec((half, HY_WIDTH), lambda i: (i, 0)),
                   pl.BlockSpec((NYQ_ROWS, HY_WIDTH), lambda i: (0, 0))],
        out_shape=[jax.ShapeDtypeStruct((T, HY_WIDTH), F32),
                   jax.ShapeDtypeStruct((T, HY_WIDTH), F32),
                   jax.ShapeDtypeStruct((NYQ_ROWS, HY_WIDTH), F32)],
        compiler_params=_cparams("arbitrary"),
        name="dft_filter",
    )(a, fe, fo)
    hr = cosp
    hi = -sinp
    hnyq = nyq[0:1]
    first = (jnp.arange(T) == 0)[:, None]
    ck = jnp.where(first, 1.0 / n, 2.0 / n)
    t1 = ck * hr
    t2 = jnp.where(first, 0.0, ck * hi)
    t4 = jnp.where(first, hnyq / n, ck * hr)
    return t1, t2, t4


def _dft_fwd_kernel(a_ref, z_ref, t1_ref, t2_ref, t4_ref, y_ref):
    half = a_ref.shape[0] // 2
    u = jnp.dot(a_ref[...], z_ref[...], preferred_element_type=F32)
    re, im = u[:half], u[half:]
    t2 = t2_ref[...]
    y_ref[:half, :] = (re * t1_ref[...] + im * t2).astype(y_ref.dtype)
    y_ref[half:, :] = (im * t4_ref[...] - re * t2).astype(y_ref.dtype)


def _dft_fwd(a, zb, t1, t2, t4):
    n, T = a.shape
    B = zb.shape[0]
    tr = _dft_tile_rows(T)
    tmap = lambda i, b: (i, 0)
    return pl.pallas_call(
        _dft_fwd_kernel,
        grid=(n // tr, B),
        in_specs=[pl.BlockSpec((tr, T), tmap),
                  pl.BlockSpec((None, T, HY_WIDTH), lambda i, b: (b, 0, 0)),
                  pl.BlockSpec((tr // 2, HY_WIDTH), tmap),
                  pl.BlockSpec((tr // 2, HY_WIDTH), tmap),
                  pl.BlockSpec((tr // 2, HY_WIDTH), tmap)],
        out_specs=pl.BlockSpec((None, tr, HY_WIDTH), lambda i, b: (b, i, 0)),
        out_shape=jax.ShapeDtypeStruct((B, n, HY_WIDTH), BF16),
        compiler_params=_cparams("parallel", "arbitrary"),
        name="dft_forward",
    )(a, zb, t1, t2, t4)


def _dft_inv_kernel(at_ref, y_ref, x0_ref, z_ref, skip_ref, o_ref):
    y = jnp.dot(at_ref[...], y_ref[...], preferred_element_type=F32)
    z = z_ref[...].astype(F32)
    o_ref[...] = (x0_ref[...].astype(F32) * (y + z * skip_ref[...])).astype(o_ref.dtype)


def _dft_inv(at, y, x0, z, skip):
    T, n = at.shape
    B = y.shape[0]
    tt = min(512, T)
    rmap = lambda i, b: (b, i, 0)
    return pl.pallas_call(
        _dft_inv_kernel,
        grid=(T // tt, B),
        in_specs=[pl.BlockSpec((tt, n), lambda i, b: (i, 0)),
                  pl.BlockSpec((None, n, HY_WIDTH), lambda i, b: (b, 0, 0)),
                  pl.BlockSpec((None, tt, HY_WIDTH), rmap),
                  pl.BlockSpec((None, tt, HY_WIDTH), rmap),
                  pl.BlockSpec((1, HY_WIDTH), lambda i, b: (0, 0))],
        out_specs=pl.BlockSpec((None, tt, HY_WIDTH), rmap),
        out_shape=jax.ShapeDtypeStruct((B, T, HY_WIDTH), BF16),
        compiler_params=_cparams("parallel", "arbitrary"),
        name="dft_inverse",
    )(at, y, x0, z, skip.reshape(1, HY_WIDTH))


def _hyena(uh, dft, spectrum, short_w, short_b, skip):
    a, at = dft
    x0, z = _hy_pre(uh, short_w, short_b)
    y = _dft_fwd(a, z, *spectrum)
    return _dft_inv(at, y, x0, z, skip)


def _outproj_kernel(ya_ref, ogq_ref, gg_ref, ohy_ref, gh_ref, x_ref, gate_ref, w_ref,
                    gqg_ref, hyg_ref, fg_ref, o_ref, *, final):
    yg = _rms_rows(ogq_ref[...].astype(F32), gqg_ref[...]) * gg_ref[...].astype(F32)
    yh = _rms_rows(ohy_ref[...].astype(F32), hyg_ref[...]) * gh_ref[...].astype(F32)
    a0, a1, a2 = DA_WIDTH, DA_WIDTH + GQ_WIDTH, D_MIX
    y = jnp.dot(ya_ref[...], w_ref[0:a0, :], preferred_element_type=F32)
    y = y + jnp.dot(yg.astype(BF16), w_ref[a0:a1, :], preferred_element_type=F32)
    y = y + jnp.dot(yh.astype(BF16), w_ref[a1:a2, :], preferred_element_type=F32)
    out = x_ref[...] + gate_ref[...] * y
    if final:
        out = _rms_rows(out, fg_ref[...])
    o_ref[...] = out


def _outproj(ya, ogq, gg, ohy, gh, xt, gate, w_bf, gq_out_g, hy_out_g, final_g, final):
    B, T, D = xt.shape
    tm = min(512, T)
    row = lambda b, i: (b, i, 0)
    c2 = lambda b, i: (0, 0)
    return pl.pallas_call(
        functools.partial(_outproj_kernel, final=final),
        grid=(B, T // tm),
        in_specs=[pl.BlockSpec((None, tm, DA_WIDTH), row),
                  pl.BlockSpec((None, tm, GQ_WIDTH), row),
                  pl.BlockSpec((None, tm, GQ_WIDTH), row),
                  pl.BlockSpec((None, tm, HY_WIDTH), row),
                  pl.BlockSpec((None, tm, HY_WIDTH), row),
                  pl.BlockSpec((None, tm, D), row),
                  pl.BlockSpec((None, 1, D), lambda b, i: (b, 0, 0)),
                  pl.BlockSpec((D_MIX, D), c2),
                  pl.BlockSpec((1, GQ_WIDTH), c2),
                  pl.BlockSpec((1, HY_WIDTH), c2),
                  pl.BlockSpec((1, D), c2)],
        out_specs=pl.BlockSpec((None, tm, D), row),
        out_shape=jax.ShapeDtypeStruct((B, T, D), F32),
        compiler_params=_cparams("parallel", "parallel"),
        name="outproj_final" if final else "outproj",
    )(ya, ogq, gg, ohy, gh, xt, gate, w_bf, gq_out_g.reshape(1, GQ_WIDTH),
      hy_out_g.reshape(1, HY_WIDTH), final_g.reshape(1, D))


def kernel(x, c, ctx, c_ctx, ada_w, ada_b, norm_g, w_in, w_out, da_lambda, da_subln_g, gq_q_g, gq_k_g, gq_out_g, hy_short_w, hy_short_b, hy_w1, hy_b1, hy_w2, hy_b2, hy_w3, hy_b3, hy_w4, hy_freq, hy_bias, hy_out_g, final_g):
    B, L, D = x.shape
    Lc = ctx.shape[1]
    depth = ada_w.shape[0]
    assert L % GRID_W == 0 and L % Lc == 0 and D_IN == w_in.shape[2]

    tabs = _rope_tables(L, DA_QK_DIM, 2) + _rope_tables(L, GQ_DIM, 1)
    dft_lat = _dft_tables(L)
    dft_ctx = _dft_tables(Lc)

    n_mod = B + 1
    pad = (-n_mod) % 8
    c_all = jnp.concatenate([c, c_ctx[None], jnp.zeros((pad, D), F32)], axis=0)
    mod = _modulation(c_all, ada_w, ada_b)

    w_in_bf = w_in.astype(BF16)
    w_out_bf = w_out.astype(BF16)

    xc = ctx
    for l in range(depth):
        update_ctx = l < depth - 1
        lam_init = 0.8 - 0.6 * math.exp(-0.3 * l)
        shift, scale, gate = (mod[l, :B, i * D:(i + 1) * D][:, None, :] for i in range(3))
        shift_c, scale_c, gate_c = (jnp.broadcast_to(mod[l, B:B + 1, i * D:(i + 1) * D][:, None, :], (B, 1, D))
                                    for i in range(3))

        qa, ka, va, ga, qg, kg, vg, gg, uh, gh = _inproj(
            x, scale, shift, norm_g[l], w_in_bf[l], gq_q_g[l], gq_k_g[l], tabs, L + Lc, 0, None)
        qa_c, ka, va, ga_c, qg_c, kg, vg, gg_c, uh_c, gh_c = _inproj(
            xc, scale_c, shift_c, norm_g[l], w_in_bf[l], gq_q_g[l], gq_k_g[l], None, L + Lc, L, (ka, va, kg, vg))

        ya = _da_attention(da_lambda[l], qa, ka, va, ga, da_subln_g[l], lam_init, L + Lc, 0)
        ogq = _gq_attention(qg, kg, vg, L + Lc, 0)

        filt = (hy_w1[l], hy_b1[l], hy_w2[l], hy_b2[l], hy_w3[l], hy_b3[l], hy_w4[l], hy_freq[l])
        spec = _filter_spectrum(dft_lat[0], *_hyena_filter_eo(L, *filt))
        ohy = _hyena(uh, dft_lat, spec, hy_short_w[l], hy_short_b[l], hy_bias[l])

        if update_ctx:
            ya_c = _da_attention(da_lambda[l], qa_c, ka, va, ga_c, da_subln_g[l], lam_init, Lc, L // Lc)
            ogq_c = _gq_attention(qg_c, kg, vg, Lc, L // Lc)
            spec_c = _filter_spectrum(dft_ctx[0], *_hyena_filter_eo(Lc, *filt))
            ohy_c = _hyena(uh_c, dft_ctx, spec_c, hy_short_w[l], hy_short_b[l], hy_bias[l])
            xc = _outproj(ya_c, ogq_c, gg_c, ohy_c, gh_c, xc, gate_c, w_out_bf[l],
                          gq_out_g[l], hy_out_g[l], final_g, False)

        x = _outproj(ya, ogq, gg, ohy, gh, x, gate, w_out_bf[l],
                     gq_out_g[l], hy_out_g[l], final_g, l == depth - 1)
    return x
```

```python
import functools
import math

import jax
import jax.numpy as jnp
import numpy as np
from jax import lax
from jax.experimental import pallas as pl
from jax.experimental.pallas import tpu as pltpu

F32 = jnp.float32
BF16 = jnp.bfloat16

GRID_W = 64
EPS = 1e-6
ROPE_THETA = 10000.0
DA_HEADS = 4
DA_QK_DIM = 64
DA_V_DIM = 2 * DA_QK_DIM
DA_WIDTH = DA_HEADS * DA_V_DIM
GQ_HEADS = 8
GQ_KV_HEADS = 2
GQ_GROUP = GQ_HEADS // GQ_KV_HEADS
GQ_DIM = 128
GQ_WIDTH = GQ_HEADS * GQ_DIM
HY_WIDTH = 512
HY_EMB = 33
HY_BANDS = (HY_EMB - 1) // 2
HY_FFN = 64
HY_MAX_DECAY = math.log(1e-2) / 0.3
HY_MIN_DECAY = math.log(1e-2) / 1.5
HY_U = 3 * HY_WIDTH
D_MIX = DA_WIDTH + GQ_WIDTH + HY_WIDTH
COL_SIZES = (DA_WIDTH, DA_WIDTH, DA_WIDTH, DA_WIDTH,
             GQ_WIDTH, GQ_KV_HEADS * GQ_DIM, GQ_KV_HEADS * GQ_DIM, GQ_WIDTH,
             HY_U, HY_WIDTH)
COL_OFFS = tuple(int(v) for v in np.cumsum((0,) + COL_SIZES))
D_IN = COL_OFFS[-1]

LANE = 128
ATTN_CHAIN_ROWS = 512
ATTN_CHAINS = 8
ATTN_LOOKAHEAD = 1
DFT_FACTOR = 64
LOG2E = math.log2(math.e)
ONES_ROWS = 16
KEY_GROUP_ROWS = 256
VMEM_LIMIT = 56 * 1024 * 1024


def _cparams(*sem):
    return pltpu.CompilerParams(dimension_semantics=sem, vmem_limit_bytes=VMEM_LIMIT)


def _silu(g):
    return g / (1.0 + jnp.exp(-g))


def _rms_rows(x, g):
    return x * lax.rsqrt(jnp.mean(x * x, axis=-1, keepdims=True) + EPS) * g


def _mod_kernel(c_ref, w_ref, b_ref, o_ref):
    c = c_ref[...]
    o_ref[...] = jnp.dot(_silu(c), w_ref[...], preferred_element_type=F32,
                         precision=lax.Precision.HIGHEST) + b_ref[...]


def _modulation(c_all, ada_w, ada_b):
    depth, d, n3 = ada_w.shape
    r = c_all.shape[0]
    tn = 512
    return pl.pallas_call(
        _mod_kernel,
        grid=(depth, n3 // tn),
        in_specs=[pl.BlockSpec((r, d), lambda l, j: (0, 0)),
                  pl.BlockSpec((None, d, tn), lambda l, j: (l, 0, j)),
                  pl.BlockSpec((None, 1, tn), lambda l, j: (l, 0, j))],
        out_specs=pl.BlockSpec((None, r, tn), lambda l, j: (l, 0, j)),
        out_shape=jax.ShapeDtypeStruct((depth, r, n3), F32),
        compiler_params=_cparams("parallel", "parallel"),
        name="modulation",
    )(c_all, ada_w, ada_b.reshape(depth, 1, n3))


def _rope_tables(L, head_dim, reps):
    rows_n = L // GRID_W
    row = jnp.repeat(jnp.arange(rows_n), GRID_W).astype(F32)
    col = jnp.tile(jnp.arange(GRID_W), rows_n).astype(F32)
    axis_dim = head_dim // 2
    inv = ROPE_THETA ** (-jnp.arange(0, axis_dim, 2, dtype=F32) / axis_dim)
    ar, ac = row[:, None] * inv[None], col[:, None] * inv[None]
    z = jnp.zeros_like(ar)
    cos = jnp.concatenate([jnp.cos(ar), jnp.cos(ar), jnp.cos(ac), jnp.cos(ac)], axis=-1)
    s_lo = jnp.concatenate([-jnp.sin(ar), z, -jnp.sin(ac), z], axis=-1)
    s_hi = jnp.concatenate([z, jnp.sin(ar), z, jnp.sin(ac)], axis=-1)
    return tuple(jnp.tile(t, (1, reps)) for t in (cos, s_lo, s_hi))


def _rope_block(x, cos, s_lo, s_hi, half):
    fwd = pltpu.roll(x, LANE - half, axis=1)
    bwd = pltpu.roll(x, half, axis=1)
    return x * cos + fwd * s_lo + bwd * s_hi


def _inproj_kernel(*refs, rope, n_alias):
    x_ref, sc_ref, sh_ref, ng_ref, w_ref, qg_g_ref, kg_g_ref = refs[:7]
    n_in = 7 + (6 if rope else 0)
    tabs = refs[7:13] if rope else None
    qa_o, ka_o, va_o, ga_o, qg_o, kg_o, vg_o, gg_o, uh_o, gh_o = refs[n_in + n_alias:]

    x = x_ref[...]
    h = _rms_rows(x, ng_ref[...])
    h = h * (1.0 + sc_ref[...]) + sh_ref[...]
    hb = h.astype(BF16)

    def proj(group):
        off = COL_OFFS[group]
        return jnp.dot(hb, w_ref[:, off:off + COL_SIZES[group]], preferred_element_type=F32)

    def blocks(v):
        return [(slice(b * LANE, (b + 1) * LANE), v[:, b * LANE:(b + 1) * LANE])
                for b in range(v.shape[1] // LANE)]

    def rope_da(v):
        if not rope:
            return v
        return _rope_block(v, tabs[0][...], tabs[1][...], tabs[2][...], DA_QK_DIM // 4)

    def rope_gq(v):
        if not rope:
            return v
        return _rope_block(v, tabs[3][...], tabs[4][...], tabs[5][...], GQ_DIM // 4)

    for sl, v in blocks(proj(0)):
        qa_o[:, sl] = (rope_da(v) * (LOG2E * DA_QK_DIM ** -0.5)).astype(qa_o.dtype)
    for sl, v in blocks(proj(1)):
        ka_o[:, sl] = rope_da(v).astype(ka_o.dtype)
    va_o[...] = proj(2).T.astype(va_o.dtype)
    ga_o[...] = _silu(proj(3)).astype(ga_o.dtype)
    for sl, v in blocks(proj(4)):
        qg_o[:, sl] = (rope_gq(_rms_rows(v, qg_g_ref[...])) * (LOG2E * GQ_DIM ** -0.5)).astype(qg_o.dtype)
    for sl, v in blocks(proj(5)):
        kg_o[:, sl] = rope_gq(_rms_rows(v, kg_g_ref[...])).astype(kg_o.dtype)
    vg_o[...] = proj(6).T.astype(vg_o.dtype)
    gg_o[...] = _silu(proj(7)).astype(gg_o.dtype)
    uh_o[...] = proj(8).astype(uh_o.dtype)
    gh_o[...] = _silu(proj(9)).astype(gh_o.dtype)


def _inproj(xt, scale, shift, norm_g, w_bf, q_g, k_g, tabs, kv_rows, kv_row0, kv_bufs):
    B, T, D = xt.shape
    tm = min(512, T)
    assert kv_row0 % tm == 0
    rope = tabs is not None
    blk0 = kv_row0 // tm
    row = lambda b, i: (b, i, 0)
    krow = lambda b, i: (b, i + blk0, 0)
    vcol = lambda b, i: (b, 0, i + blk0)
    const2 = lambda b, i: (0, 0)
    in_specs = [pl.BlockSpec((None, tm, D), row),
                pl.BlockSpec((None, 1, D), lambda b, i: (b, 0, 0)),
                pl.BlockSpec((None, 1, D), lambda b, i: (b, 0, 0)),
                pl.BlockSpec((1, D), const2),
                pl.BlockSpec((D, D_IN), const2, pipeline_mode=pl.Buffered(1)),
                pl.BlockSpec((1, GQ_DIM), const2),
                pl.BlockSpec((1, GQ_DIM), const2)]
    args = [xt, scale, shift, norm_g.reshape(1, D), w_bf, q_g.reshape(1, GQ_DIM), k_g.reshape(1, GQ_DIM)]
    if rope:
        in_specs += [pl.BlockSpec((tm, LANE), lambda b, i: (i, 0))] * 6
        args += list(tabs)
    aliases = {}
    if kv_bufs is not None:
        kv_out_index = (1, 2, 5, 6)
        aliases = {len(args) + n: o for n, o in enumerate(kv_out_index)}
        in_specs += [pl.BlockSpec(memory_space=pl.ANY)] * len(kv_bufs)
        args += list(kv_bufs)
    kvw = GQ_KV_HEADS * GQ_DIM
    tok = lambda w, dt: (pl.BlockSpec((None, tm, w), row), jax.ShapeDtypeStruct((B, T, w), dt))
    keys = lambda w: (pl.BlockSpec((None, tm, w), krow), jax.ShapeDtypeStruct((B, kv_rows, w), BF16))
    vals = lambda w: (pl.BlockSpec((None, w, tm), vcol), jax.ShapeDtypeStruct((B, w, kv_rows), BF16))
    outs = [tok(DA_WIDTH, BF16), keys(DA_WIDTH), vals(DA_WIDTH), tok(DA_WIDTH, BF16),
            tok(GQ_WIDTH, BF16), keys(kvw), vals(kvw), tok(GQ_WIDTH, BF16),
            tok(HY_U, BF16), tok(HY_WIDTH, BF16)]
    return pl.pallas_call(
        functools.partial(_inproj_kernel, rope=rope, n_alias=len(aliases)),
        grid=(B, T // tm),
        in_specs=in_specs,
        out_specs=[o[0] for o in outs],
        out_shape=[o[1] for o in outs],
        input_output_aliases=aliases,
        compiler_params=_cparams("parallel", "parallel"),
        name="inproj_rope" if rope else "inproj_ctx",
    )(*args)


def _reduce_keys(a, op):
    tk, m = a.shape
    g = KEY_GROUP_ROWS if tk % KEY_GROUP_ROWS == 0 else tk
    part = op(a.reshape(tk // g, g, m), axis=0)
    return op(part, axis=0, keepdims=True)


def _scores_t(qs, k):
    return lax.dot_general(k, qs, (((1,), (1,)), ((), ())), preferred_element_type=F32)


def _softmax_pv_t(st, vt):
    m = _reduce_keys(st, jnp.max)
    p = jnp.exp2(st - m).astype(BF16)
    ot = jnp.dot(vt, p, preferred_element_type=F32)
    return (ot[:LANE] / ot[LANE:LANE + 1]).T


def _attend_chains(queries, k_ref, vt_ref):
    n = len(queries)
    vt = jnp.concatenate([vt_ref[...], jnp.ones((ONES_ROWS, vt_ref.shape[1]), BF16)], axis=0)
    scores = [_scores_t(queries[j](), k_ref[...]) for j in range(min(ATTN_LOOKAHEAD, n))]
    outs = []
    for i in range(n):
        if i + ATTN_LOOKAHEAD < n:
            scores.append(_scores_t(queries[i + ATTN_LOOKAHEAD](), k_ref[...]))
        outs.append(_softmax_pv_t(scores[i], vt))
        scores[i] = None
    return outs


def _da_kernel(lam_ref, q_ref, k_ref, v_ref, gate_ref, sg_ref, o_ref, *, lam_init):
    tq = q_ref.shape[0]
    lm = lam_ref[...]
    lam = (jnp.exp(jnp.sum(lm[0:1] * lm[1:2], axis=-1, keepdims=True))
           - jnp.exp(jnp.sum(lm[2:3] * lm[3:4], axis=-1, keepdims=True)) + lam_init)
    ch = min(ATTN_CHAIN_ROWS, tq)

    def component(r, c):
        def load():
            q = q_ref[r * ch:(r + 1) * ch, :]
            lane = lax.broadcasted_iota(jnp.int32, q.shape, 1)
            keep = (lane < DA_QK_DIM) if c == 0 else (lane >= DA_QK_DIM)
            return jnp.where(keep, q, jnp.zeros_like(q))
        return load

    n = tq // ch
    outs = _attend_chains([component(r, c) for r in range(n) for c in range(2)], k_ref, v_ref)
    for r in range(n):
        rows = slice(r * ch, (r + 1) * ch)
        o = _rms_rows(outs[2 * r] - lam * outs[2 * r + 1], sg_ref[...]) * (1.0 - lam_init)
        o_ref[rows, :] = (o * gate_ref[rows, :].astype(F32)).astype(o_ref.dtype)


def _da_attention(lam_p, q, k, v, gate, subln_g, lam_init, Tk, kblk):
    B, Tq, _ = q.shape
    tq = min(ATTN_CHAINS * ATTN_CHAIN_ROWS // 2, Tq)
    qmap = lambda b, h, i: (b, i, h)
    kmap = lambda b, h, i: (b, kblk, h)
    vmap = lambda b, h, i: (b, h, kblk)
    return pl.pallas_call(
        functools.partial(_da_kernel, lam_init=lam_init),
        grid=(B, DA_HEADS, Tq // tq),
        in_specs=[pl.BlockSpec((4, DA_QK_DIM), lambda b, h, i: (0, 0)),
                  pl.BlockSpec((None, tq, LANE), qmap),
                  pl.BlockSpec((None, Tk, LANE), kmap),
                  pl.BlockSpec((None, LANE, Tk), vmap),
                  pl.BlockSpec((None, tq, LANE), qmap),
                  pl.BlockSpec((1, LANE), lambda b, h, i: (0, 0))],
        out_specs=pl.BlockSpec((None, tq, LANE), qmap),
        out_shape=jax.ShapeDtypeStruct((B, Tq, DA_WIDTH), BF16),
        compiler_params=_cparams("parallel", "parallel", "arbitrary"),
        name="diff_attention",
    )(lam_p, q, k, v, gate, subln_g.reshape(1, DA_V_DIM))


def _gq_kernel(q_ref, k_ref, v_ref, o_ref):
    tq = q_ref.shape[0]
    ch = min(ATTN_CHAIN_ROWS, tq)
    where = [(slice(c * ch, (c + 1) * ch), slice(r * LANE, (r + 1) * LANE))
             for c in range(tq // ch) for r in range(GQ_GROUP)]
    outs = _attend_chains([functools.partial(lambda w: q_ref[w[0], w[1]], w) for w in where], k_ref, v_ref)
    for (rows, sl), o in zip(where, outs):
        o_ref[rows, sl] = o.astype(o_ref.dtype)


def _gq_attention(q, k, v, Tk, kblk):
    B, Tq, _ = q.shape
    tq = min(ATTN_CHAINS * ATTN_CHAIN_ROWS // GQ_GROUP, Tq)
    gw = GQ_GROUP * GQ_DIM
    qmap = lambda b, g, i: (b, i, g)
    kmap = lambda b, g, i: (b, kblk, g)
    vmap = lambda b, g, i: (b, g, kblk)
    return pl.pallas_call(
        _gq_kernel,
        grid=(B, GQ_KV_HEADS, Tq // tq),
        in_specs=[pl.BlockSpec((None, tq, gw), qmap),
                  pl.BlockSpec((None, Tk, LANE), kmap),
                  pl.BlockSpec((None, LANE, Tk), vmap)],
        out_specs=pl.BlockSpec((None, tq, gw), qmap),
        out_shape=jax.ShapeDtypeStruct((B, Tq, GQ_WIDTH), BF16),
        compiler_params=_cparams("parallel", "parallel", "arbitrary"),
        name="gqa_attention",
    )(q, k, v)


def _hy_pre_kernel(u0_ref, u1_ref, u2_ref, w0_ref, w1_ref, w2_ref, b0_ref, b1_ref, b2_ref,
                   x0_ref, z_ref):
    T = u0_ref.shape[0]
    row = lax.broadcasted_iota(jnp.int32, u0_ref.shape, 0)

    def sconv(u_ref, w_ref, b_ref):
        u = u_ref[...].astype(F32)
        w = w_ref[...]
        prev = jnp.where(row == 0, 0.0, pltpu.roll(u, 1, axis=0))
        nxt = jnp.where(row == T - 1, 0.0, pltpu.roll(u, T - 1, axis=0))
        return prev * w[0:1] + u * w[1:2] + nxt * w[2:3] + b_ref[...]

    x0_ref[...] = sconv(u0_ref, w0_ref, b0_ref).astype(x0_ref.dtype)
    z_ref[...] = (sconv(u2_ref, w2_ref, b2_ref) * sconv(u1_ref, w1_ref, b1_ref)).astype(z_ref.dtype)


def _hy_pre(uh, short_w, short_b):
    B, T, _ = uh.shape
    nb = HY_WIDTH // LANE
    u_specs = [pl.BlockSpec((None, T, LANE), lambda b, j, p=p: (b, 0, p * nb + j)) for p in range(3)]
    w_specs = [pl.BlockSpec((3, LANE), lambda b, j, p=p: (0, p * nb + j)) for p in range(3)]
    b_specs = [pl.BlockSpec((1, LANE), lambda b, j, p=p: (0, p * nb + j)) for p in range(3)]
    out_spec = pl.BlockSpec((None, T, LANE), lambda b, j: (b, 0, j))
    sb = short_b.reshape(1, HY_U)
    return pl.pallas_call(
        _hy_pre_kernel,
        grid=(B, nb),
        in_specs=u_specs + w_specs + b_specs,
        out_specs=[out_spec] * 2,
        out_shape=[jax.ShapeDtypeStruct((B, T, HY_WIDTH), BF16)] * 2,
        compiler_params=_cparams("parallel", "parallel"),
        name="hyena_short_conv",
    )(uh, uh, uh, short_w, short_w, short_w, sb, sb, sb)


def _dot_3pass(a, b):
    a_hi = a.astype(BF16)
    b_hi = b.astype(BF16)
    a_lo = (a - a_hi.astype(F32)).astype(BF16)
    b_lo = (b - b_hi.astype(F32)).astype(BF16)
    dot = functools.partial(jnp.dot, preferred_element_type=F32)
    return dot(a_hi, b_hi) + (dot(a_hi, b_lo) + dot(a_lo, b_hi))


def _filter_kernel(z_ref, w1_ref, b1_ref, w2_ref, b2_ref, w3_ref, b3_ref, w4f_ref, w4b_ref,
                   fr_ref, t_ref, dl_ref, fe_ref, fo_ref, h_ref):
    @pl.when(pl.program_id(0) == 0)
    def _():
        fr = fr_ref[...]
        h = jnp.sin(fr * (_dot_3pass(z_ref[...], w1_ref[...]) + b1_ref[...]))
        h = jnp.sin(fr * (_dot_3pass(h, w2_ref[...]) + b2_ref[...]))
        h_ref[...] = jnp.sin(fr * (_dot_3pass(h, w3_ref[...]) + b3_ref[...]))

    h = h_ref[...]
    decay = jnp.exp(-t_ref[...] * jnp.abs(dl_ref[...]))
    fwd = _dot_3pass(h, w4f_ref[...]) * decay
    bwd = _dot_3pass(h, w4b_ref[...]) * decay
    row = lax.broadcasted_iota(jnp.int32, bwd.shape, 0)
    bwd = jnp.where(row == 0, 0.0, bwd)
    ss = jnp.sum(fwd * fwd + bwd * bwd, axis=0, keepdims=True)
    sc = lax.rsqrt(ss + EPS)
    fe_ref[...] = ((fwd + bwd) * sc).astype(fe_ref.dtype)
    fo_ref[...] = ((fwd - bwd) * sc).astype(fo_ref.dtype)


def _hyena_filter_eo(T, w1, b1, w2, b2, w3, b3, w4, freq):
    t = jnp.linspace(0.0, 1.0, T, dtype=F32)[:, None]
    w = 2.0 * math.pi * jnp.arange(T, dtype=F32)[:, None] / T
    f = jnp.linspace(1e-4, HY_BANDS - 1, HY_BANDS, dtype=F32)[None]
    z = jnp.concatenate([t, jnp.cos(f * w), -jnp.sin(f * w)], axis=-1)
    z = jnp.pad(z, ((0, 0), (0, LANE - HY_EMB)))
    w1p = jnp.pad(w1, ((0, LANE - HY_EMB), (0, 0)))
    deltas = jnp.linspace(HY_MIN_DECAY, HY_MAX_DECAY, HY_WIDTH, dtype=F32)[None]
    nb = HY_WIDTH // LANE
    c2 = lambda j: (0, 0)
    vec = lambda a: a.reshape(1, HY_FFN)
    return pl.pallas_call(
        _filter_kernel,
        grid=(nb,),
        in_specs=[pl.BlockSpec((T, LANE), c2),
                  pl.BlockSpec((LANE, HY_FFN), c2), pl.BlockSpec((1, HY_FFN), c2),
                  pl.BlockSpec((HY_FFN, HY_FFN), c2), pl.BlockSpec((1, HY_FFN), c2),
                  pl.BlockSpec((HY_FFN, HY_FFN), c2), pl.BlockSpec((1, HY_FFN), c2),
                  pl.BlockSpec((HY_FFN, LANE), lambda j: (0, j)),
                  pl.BlockSpec((HY_FFN, LANE), lambda j: (0, nb + j)),
                  pl.BlockSpec((1, HY_FFN), c2),
                  pl.BlockSpec((T, 1), c2),
                  pl.BlockSpec((1, LANE), lambda j: (0, j))],
        out_specs=[pl.BlockSpec((T, LANE), lambda j: (0, j))] * 2,
        out_shape=[jax.ShapeDtypeStruct((T, HY_WIDTH), BF16)] * 2,
        scratch_shapes=[pltpu.VMEM((T, HY_FFN), F32)],
        compiler_params=_cparams("arbitrary"),
        name="hyena_filter",
    )(z, w1p, vec(b1), w2, vec(b2), w3, vec(b3), w4, w4, vec(freq), t, deltas)


def _dft_tile_rows(T):
    return min(1024, 2 * T)


def _dft_tables(T):
    n = 2 * T
    tr = _dft_tile_rows(T)
    half = tr // 2
    f = DFT_FACTOR
    assert T % f == 0 and half % f == 0
    ang = 2.0 * math.pi / n

    def cs(k, s):
        a = ((k * s) % n).astype(F32) * ang
        return jnp.cos(a), jnp.sin(a)

    s = jnp.arange(T, dtype=jnp.int32)[None]
    cl, sl = cs(jnp.arange(f, dtype=jnp.int32)[:, None], s)
    ch, sh = cs(jnp.arange(T // f, dtype=jnp.int32)[:, None] * f, s)
    x = jnp.stack([cl, sl])[None, :, None]
    w = jnp.stack([-sl, cl])[None, :, None]
    ch5 = ch.reshape(n // tr, 1, half // f, 1, T)
    sh5 = sh.reshape(n // tr, 1, half // f, 1, T)
    a = (ch5 * x + sh5 * w).reshape(n, T).astype(BF16)
    nyq = jnp.where(jnp.arange(T) % 2 == 0, 1.0, -1.0).astype(BF16)
    a = a.at[half].set(nyq)

    r = np.arange(n)
    local = r % tr
    k = (r // tr) * half + local % half
    is_sin = local >= half
    is_nyq = is_sin & (k == 0)
    k = jnp.asarray(np.where(is_nyq, T, k), jnp.int32)[None]
    sel = jnp.asarray(is_sin & ~is_nyq)[None]
    c_lo, s_lo = cs(k, jnp.arange(f, dtype=jnp.int32)[:, None])
    c_hi, s_hi = cs(k, jnp.arange(T // f, dtype=jnp.int32)[:, None] * f)
    x_lo = jnp.where(sel, s_lo, c_lo)
    w_lo = jnp.where(sel, c_lo, -s_lo)
    at = (x_lo[None] * c_hi[:, None] + w_lo[None] * s_hi[:, None]).reshape(T, n).astype(BF16)
    return a, at


NYQ_ROWS = 16


def _dft_filter_kernel(a_ref, fe_ref, fo_ref, c_ref, s_ref, nyq_ref):
    half = a_ref.shape[0] // 2
    c_ref[...] = jnp.dot(a_ref[:half, :], fe_ref[...], preferred_element_type=F32)
    s_ref[...] = jnp.dot(a_ref[half:, :], fo_ref[...], preferred_element_type=F32)

    @pl.when(pl.program_id(0) == 0)
    def _():
        nyq_ref[...] = jnp.dot(a_ref[half:half + NYQ_ROWS, :], fe_ref[...], preferred_element_type=F32)


def _filter_spectrum(a, fe, fo):
    n, T = a.shape
    tr = _dft_tile_rows(T)
    half = tr // 2
    cosp, sinp, nyq = pl.pallas_call(
        _dft_filter_kernel,
        grid=(n // tr,),
        in_specs=[pl.BlockSpec((tr, T), lambda i: (i, 0)),
                  pl.BlockSpec((T, HY_WIDTH), lambda i: (0, 0)),
                  pl.BlockSpec((T, HY_WIDTH), lambda i: (0, 0))],
        out_specs=[pl.BlockSpec((half, HY_WIDTH), lambda i: (i, 0)),
                   pl.BlockSpec((half, HY_WIDTH), lambda i: (i, 0)),
                   pl.BlockSpec((NYQ_ROWS, HY_WIDTH), lambda i: (0, 0))],
        out_shape=[jax.ShapeDtypeStruct((T, HY_WIDTH), F32),
                   jax.ShapeDtypeStruct((T, HY_WIDTH), F32),
                   jax.ShapeDtypeStruct((NYQ_ROWS, HY_WIDTH), F32)],
        compiler_params=_cparams("arbitrary"),
        name="dft_filter",
    )(a, fe, fo)
    hr = cosp
    hi = -sinp
    hnyq = nyq[0:1]
    first = (jnp.arange(T) == 0)[:, None]
    ck = jnp.where(first, 1.0 / n, 2.0 / n)
    t1 = ck * hr
    t2 = jnp.where(first, 0.0, ck * hi)
    t4 = jnp.where(first, hnyq / n, ck * hr)
    return t1, t2, t4


def _dft_fwd_kernel(a_ref, z_ref, t1_ref, t2_ref, t4_ref, y_ref):
    half = a_ref.shape[0] // 2
    u = jnp.dot(a_ref[...], z_ref[...], preferred_element_type=F32)
    re, im = u[:half], u[half:]
    t2 = t2_ref[...]
    y_ref[:half, :] = (re * t1_ref[...] + im * t2).astype(y_ref.dtype)
    y_ref[half:, :] = (im * t4_ref[...] - re * t2).astype(y_ref.dtype)


def _dft_fwd(a, zb, t1, t2, t4):
    n, T = a.shape
    B = zb.shape[0]
    tr = _dft_tile_rows(T)
    tmap = lambda i, b: (i, 0)
    return pl.pallas_call(
        _dft_fwd_kernel,
        grid=(n // tr, B),
        in_specs=[pl.BlockSpec((tr, T), tmap),
                  pl.BlockSpec((None, T, HY_WIDTH), lambda i, b: (b, 0, 0)),
                  pl.BlockSpec((tr // 2, HY_WIDTH), tmap),
                  pl.BlockSpec((tr // 2, HY_WIDTH), tmap),
                  pl.BlockSpec((tr // 2, HY_WIDTH), tmap)],
        out_specs=pl.BlockSpec((None, tr, HY_WIDTH), lambda i, b: (b, i, 0)),
        out_shape=jax.ShapeDtypeStruct((B, n, HY_WIDTH), BF16),
        compiler_params=_cparams("parallel", "arbitrary"),
        name="dft_forward",
    )(a, zb, t1, t2, t4)


def _dft_inv_kernel(at_ref, y_ref, x0_ref, z_ref, skip_ref, o_ref):
    y = jnp.dot(at_ref[...], y_ref[...], preferred_element_type=F32)
    z = z_ref[...].astype(F32)
    o_ref[...] = (x0_ref[...].astype(F32) * (y + z * skip_ref[...])).astype(o_ref.dtype)


def _dft_inv(at, y, x0, z, skip):
    T, n = at.shape
    B = y.shape[0]
    tt = min(512, T)
    rmap = lambda i, b: (b, i, 0)
    return pl.pallas_call(
        _dft_inv_kernel,
        grid=(T // tt, B),
        in_specs=[pl.BlockSpec((tt, n), lambda i, b: (i, 0)),
                  pl.BlockSpec((None, n, HY_WIDTH), lambda i, b: (b, 0, 0)),
                  pl.BlockSpec((None, tt, HY_WIDTH), rmap),
                  pl.BlockSpec((None, tt, HY_WIDTH), rmap),
                  pl.BlockSpec((1, HY_WIDTH), lambda i, b: (0, 0))],
        out_specs=pl.BlockSpec((None, tt, HY_WIDTH), rmap),
        out_shape=jax.ShapeDtypeStruct((B, T, HY_WIDTH), BF16),
        compiler_params=_cparams("parallel", "arbitrary"),
        name="dft_inverse",
    )(at, y, x0, z, skip.reshape(1, HY_WIDTH))


def _hyena(uh, dft, spectrum, short_w, short_b, skip):
    a, at = dft
    x0, z = _hy_pre(uh, short_w, short_b)
    y = _dft_fwd(a, z, *spectrum)
    return _dft_inv(at, y, x0, z, skip)


def _outproj_kernel(ya_ref, ogq_ref, gg_ref, ohy_ref, gh_ref, x_ref, gate_ref, w_ref,
                    gqg_ref, hyg_ref, fg_ref, o_ref, *, final):
    yg = _rms_rows(ogq_ref[...].astype(F32), gqg_ref[...]) * gg_ref[...].astype(F32)
    yh = _rms_rows(ohy_ref[...].astype(F32), hyg_ref[...]) * gh_ref[...].astype(F32)
    a0, a1, a2 = DA_WIDTH, DA_WIDTH + GQ_WIDTH, D_MIX
    y = jnp.dot(ya_ref[...], w_ref[0:a0, :], preferred_element_type=F32)
    y = y + jnp.dot(yg.astype(BF16), w_ref[a0:a1, :], preferred_element_type=F32)
    y = y + jnp.dot(yh.astype(BF16), w_ref[a1:a2, :], preferred_element_type=F32)
    out = x_ref[...] + gate_ref[...] * y
    if final:
        out = _rms_rows(out, fg_ref[...])
    o_ref[...] = out


def _outproj(ya, ogq, gg, ohy, gh, xt, gate, w_bf, gq_out_g, hy_out_g, final_g, final):
    B, T, D = xt.shape
    tm = min(512, T)
    row = lambda b, i: (b, i, 0)
    c2 = lambda b, i: (0, 0)
    return pl.pallas_call(
        functools.partial(_outproj_kernel, final=final),
        grid=(B, T // tm),
        in_specs=[pl.BlockSpec((None, tm, DA_WIDTH), row),
                  pl.BlockSpec((None, tm, GQ_WIDTH), row),
                  pl.BlockSpec((None, tm, GQ_WIDTH), row),
                  pl.BlockSpec((None, tm, HY_WIDTH), row),
                  pl.BlockSpec((None, tm, HY_WIDTH), row),
                  pl.BlockSpec((None, tm, D), row),
                  pl.BlockSpec((None, 1, D), lambda b, i: (b, 0, 0)),
                  pl.BlockSpec((D_MIX, D), c2),
                  pl.BlockSpec((1, GQ_WIDTH), c2),
                  pl.BlockSpec((1, HY_WIDTH), c2),
                  pl.BlockSpec((1, D), c2)],
        out_specs=pl.BlockSpec((None, tm, D), row),
        out_shape=jax.ShapeDtypeStruct((B, T, D), F32),
        compiler_params=_cparams("parallel", "parallel"),
        name="outproj_final" if final else "outproj",
    )(ya, ogq, gg, ohy, gh, xt, gate, w_bf, gq_out_g.reshape(1, GQ_WIDTH),
      hy_out_g.reshape(1, HY_WIDTH), final_g.reshape(1, D))


def kernel(x, c, ctx, c_ctx, ada_w, ada_b, norm_g, w_in, w_out, da_lambda, da_subln_g, gq_q_g, gq_k_g, gq_out_g, hy_short_w, hy_short_b, hy_w1, hy_b1, hy_w2, hy_b2, hy_w3, hy_b3, hy_w4, hy_freq, hy_bias, hy_out_g, final_g):
    B, L, D = x.shape
    Lc = ctx.shape[1]
    depth = ada_w.shape[0]
    assert L % GRID_W == 0 and L % Lc == 0 and D_IN == w_in.shape[2]

    tabs = _rope_tables(L, DA_QK_DIM, 2) + _rope_tables(L, GQ_DIM, 1)
    dft_lat = _dft_tables(L)
    dft_ctx = _dft_tables(Lc)

    n_mod = B + 1
    pad = (-n_mod) % 8
    c_all = jnp.concatenate([c, c_ctx[None], jnp.zeros((pad, D), F32)], axis=0)
    mod = _modulation(c_all, ada_w, ada_b)

    w_in_bf = w_in.astype(BF16)
    w_out_bf = w_out.astype(BF16)

    xc = ctx
    for l in range(depth):
        update_ctx = l < depth - 1
        lam_init = 0.8 - 0.6 * math.exp(-0.3 * l)
        shift, scale, gate = (mod[l, :B, i * D:(i + 1) * D][:, None, :] for i in range(3))
        shift_c, scale_c, gate_c = (jnp.broadcast_to(mod[l, B:B + 1, i * D:(i + 1) * D][:, None, :], (B, 1, D))
                                    for i in range(3))

        qa, ka, va, ga, qg, kg, vg, gg, uh, gh = _inproj(
            x, scale, shift, norm_g[l], w_in_bf[l], gq_q_g[l], gq_k_g[l], tabs, L + Lc, 0, None)
        qa_c, ka, va, ga_c, qg_c, kg, vg, gg_c, uh_c, gh_c = _inproj(
            xc, scale_c, shift_c, norm_g[l], w_in_bf[l], gq_q_g[l], gq_k_g[l], None, L + Lc, L, (ka, va, kg, vg))

        ya = _da_attention(da_lambda[l], qa, ka, va, ga, da_subln_g[l], lam_init, L + Lc, 0)
        ogq = _gq_attention(qg, kg, vg, L + Lc, 0)

        filt = (hy_w1[l], hy_b1[l], hy_w2[l], hy_b2[l], hy_w3[l], hy_b3[l], hy_w4[l], hy_freq[l])
        spec = _filter_spectrum(dft_lat[0], *_hyena_filter_eo(L, *filt))
        ohy = _hyena(uh, dft_lat, spec, hy_short_w[l], hy_short_b[l], hy_bias[l])

        if update_ctx:
            ya_c = _da_attention(da_lambda[l], qa_c, ka, va, ga_c, da_subln_g[l], lam_init, Lc, L // Lc)
            ogq_c = _gq_attention(qg_c, kg, vg, Lc, L // Lc)
            spec_c = _filter_spectrum(dft_ctx[0], *_hyena_filter_eo(Lc, *filt))
            ohy_c = _hyena(uh_c, dft_ctx, spec_c, hy_short_w[l], hy_short_b[l], hy_bias[l])
            xc = _outproj(ya_c, ogq_c, gg_c, ohy_c, gh_c, xc, gate_c, w_out_bf[l],
                          gq_out_g[l], hy_out_g[l], final_g, False)

        x = _outproj(ya, ogq, gg, ohy, gh, x, gate, w_out_bf[l],
                     gq_out_g[l], hy_out_g[l], final_g, l == depth - 1)
    return x
```
